```python
import jax, jax.numpy as jnp
from jax import lax
import numpy as np

D_MODEL = 2048
BATCH = 32
SEQ = 256
DEPTH = 1
DEC_BATCH = 2
DEC_SEQ = 4096
PAST_LEN = 512

GRID_W = 64
HEAD_DIM = 64
D_A = D_MODEL
N_HEADS_A = D_A // HEAD_DIM
D_B = D_MODEL // 2
N_GROUPS_B = 4
GROUP_B = D_B // N_GROUPS_B
LORA_W = 64
LORA_A = 64
LORA_G = 128
D_FF = ((8 * D_MODEL // 3 + 127) // 128) * 128
CONV_W = 3
N_DIRS = 2
N_MOD = 6
RMS_EPS = 1e-6
GN_EPS = 64e-5
IN_COLS = 3 * D_A + D_B + 2 * D_MODEL + LORA_W + LORA_A + LORA_G

kernel_name = 'bidir_rwkv7_fnet_convffn_prefix_dit'


def _rms(x, g):
    xf = x.astype(jnp.float32)
    y = xf * lax.rsqrt(jnp.mean(xf * xf, axis=-1, keepdims=True) + RMS_EPS)
    return (y * g.astype(jnp.float32)).astype(x.dtype)


def _dwconv3(x, w, b, n_rows, row_len):
    B, T, C = x.shape
    xr = x.reshape(B, n_rows, row_len, C)
    xp = jnp.pad(xr, ((0, 0), (0, 0), (1, 1), (0, 0)))
    y = xp[:, :, :-2] * w[0] + xp[:, :, 1:-1] * w[1] + xp[:, :, 2:] * w[2] + b
    return y.reshape(B, T, C)


def _fourier(xb):
    B, T, _ = xb.shape
    xg = xb.astype(jnp.float32).reshape(B, T, N_GROUPS_B, GROUP_B)
    y = jnp.fft.fft2(xg, axes=(1, 3), norm='ortho').real
    return y.reshape(B, T, D_B).astype(xb.dtype)


def _wkv_scan(s0, r, w, k, v, kk, a):
    def step(S, inp):
        r_t, w_t, k_t, v_t, kk_t, a_t = inp
        sa = jnp.einsum('bhvk,bhk->bhv', S, -kk_t)
        S = (S * w_t[:, :, None, :] + sa[..., None] * (kk_t * a_t)[:, :, None, :]
             + v_t[..., None] * k_t[:, :, None, :])
        y = jnp.einsum('bhvk,bhk->bhv', S, r_t)
        return S, y
    xs = tuple(jnp.moveaxis(t, 1, 0) for t in (r, w, k, v, kk, a))
    S, ys = lax.scan(step, s0, xs)
    return jnp.moveaxis(ys, 0, 1), S


def _rwkv7_mixer(r, k, v, dw, da, dg, s0, decay_up, decay_base, iclr_up, iclr_base,
                 gate_up, k_k, k_a, r_k, lnx_g, lnx_b):
    f32 = jnp.float32
    B, T, _ = r.shape

    def heads(t):
        return t.astype(f32).reshape(t.shape[:-1] + (N_HEADS_A, HEAD_DIM))

    r_h, k_h, v_h = heads(r), heads(k), heads(v)
    kk = k_h * heads(k_k)
    kk = kk * lax.rsqrt(jnp.sum(kk * kk, axis=-1, keepdims=True) + 1e-12)
    k_a_h = heads(k_a)
    tw = jnp.tanh(dw.astype(f32))
    daf = da.astype(f32)
    ys = []
    states = []
    for d in range(N_DIRS):
        w_logit = decay_base[d].astype(f32) + tw @ decay_up[d].astype(f32)
        decay = jnp.exp(-jnp.exp(-jax.nn.softplus(-w_logit) - 0.5))
        a_h = heads(jax.nn.sigmoid(iclr_base[d].astype(f32) + daf @ iclr_up[d].astype(f32)))
        k_d = k_h * (1.0 + (a_h - 1.0) * k_a_h)
        seq = (r_h, heads(decay), k_d, v_h, kk, a_h)
        if d == 1:
            seq = tuple(jnp.flip(t, axis=1) for t in seq)
        y_d, s_d = _wkv_scan(s0[:, d].astype(f32), *seq)
        if d == 1:
            y_d = jnp.flip(y_d, axis=1)
        ys.append(y_d)
        states.append(s_d)
    y = ys[0] + ys[1]
    mu = jnp.mean(y, axis=-1, keepdims=True)
    var = jnp.mean(jnp.square(y - mu), axis=-1, keepdims=True)
    y = (y - mu) * lax.rsqrt(var + GN_EPS) * heads(lnx_g) + heads(lnx_b)
    bonus = jnp.sum(r_h * k_h * heads(r_k), axis=-1, keepdims=True) * v_h
    g = jax.nn.sigmoid(dg.astype(f32)) @ gate_up.astype(f32)
    out = ((y + bonus).reshape(B, T, D_A) * g).astype(r.dtype)
    return out, jnp.stack(states, axis=1)


def _layer(x, mod, s0, n_rows, row_len, lp):
    (norm_mix_g, w_in, rkv_conv_w, rkv_conv_b, decay_up, decay_base, iclr_up, iclr_base,
     gate_up, k_k, k_a, r_k, lnx_g, lnx_b, w_out_a, w_fourier, w_out, norm_ffn_g,
     ffn_w_in, ffn_conv_w, ffn_conv_b, ffn_w_down) = lp
    sh1, sc1, gt1, sh2, sc2, gt2 = jnp.split(mod[:, None, :], N_MOD, axis=-1)
    h = _rms(x, norm_mix_g) * (1.0 + sc1) + sh1
    proj = h @ w_in
    c1 = 3 * D_A
    c2 = c1 + D_B
    c3 = c2 + 2 * D_MODEL
    c4 = c3 + LORA_W
    c5 = c4 + LORA_A
    rkv, xb, gates, dw, da, dg = jnp.split(proj, [c1, c2, c3, c4, c5], axis=-1)
    rkv = _dwconv3(rkv, rkv_conv_w, rkv_conv_b, n_rows, row_len)
    r, k, v = jnp.split(rkv, 3, axis=-1)
    y_a, s_fin = _rwkv7_mixer(r, k, v, dw, da, dg, s0, decay_up, decay_base, iclr_up,
                              iclr_base, gate_up, k_k, k_a, r_k, lnx_g, lnx_b)
    y_b = _fourier(xb) @ w_fourier
    g_a, g_b = jnp.split(jax.nn.sigmoid(gates), 2, axis=-1)
    mixed = (g_a * (y_a @ w_out_a) + g_b * y_b) @ w_out
    x = x + gt1 * mixed
    h2 = _rms(x, norm_ffn_g) * (1.0 + sc2) + sh2
    u = _dwconv3(h2 @ ffn_w_in, ffn_conv_w, ffn_conv_b, n_rows, row_len)
    u_gate, u_val = jnp.split(u, 2, axis=-1)
    x = x + gt2 * ((jax.nn.silu(u_gate) * u_val) @ ffn_w_down)
    return x, s_fin


def setup_inputs(seed: int = 0) -> dict:
    key = jax.random.key(seed)
    ks = iter(jax.random.split(key, 40))

    def nrm(shape, scale):
        return jax.random.normal(next(ks), shape, jnp.float32) * scale

    L = DEPTH
    return {
        'x_prompt': nrm((BATCH, SEQ, D_MODEL), 1.0),
        'x_sample': nrm((DEC_BATCH, DEC_SEQ, D_MODEL), 1.0),
        'state_rwkv': nrm((DEC_BATCH, L, N_DIRS, N_HEADS_A, HEAD_DIM, HEAD_DIM), 0.3),
        'c': nrm((DEC_BATCH, D_MODEL), 1.0),
        'c_ctx': nrm((D_MODEL,), 1.0),
        'ada_w': nrm((L, D_MODEL, N_MOD * D_MODEL), 0.5 * D_MODEL ** -0.5),
        'ada_b': nrm((L, N_MOD * D_MODEL), 0.02),
        'norm_mix_g': 1.0 + nrm((L, D_MODEL), 0.02),
        'w_in': nrm((L, D_MODEL, IN_COLS), D_MODEL ** -0.5),
        'rkv_conv_w': nrm((L, CONV_W, 3 * D_A), CONV_W ** -0.5),
        'rkv_conv_b': nrm((L, 3 * D_A), 0.02),
        'decay_up': nrm((L, N_DIRS, LORA_W, D_A), LORA_W ** -0.5),
        'decay_base': nrm((L, N_DIRS, D_A), 0.5),
        'iclr_up': nrm((L, N_DIRS, LORA_A, D_A), LORA_A ** -0.5),
        'iclr_base': nrm((L, N_DIRS, D_A), 0.5),
        'gate_up': nrm((L, LORA_G, D_A), LORA_G ** -0.5),
        'k_k': 0.85 + nrm((L, D_A), 0.02),
        'k_a': 1.0 + nrm((L, D_A), 0.02),
        'r_k': nrm((L, D_A), 0.1),
        'lnx_g': 1.0 + nrm((L, D_A), 0.02),
        'lnx_b': nrm((L, D_A), 0.02),
        'w_out_a': nrm((L, D_A, D_MODEL), D_A ** -0.5),
        'w_fourier': nrm((L, D_B, D_MODEL), D_B ** -0.5),
        'w_out': nrm((L, D_MODEL, D_MODEL), D_MODEL ** -0.5),
        'norm_ffn_g': 1.0 + nrm((L, D_MODEL), 0.02),
        'ffn_w_in': nrm((L, D_MODEL, 2 * D_FF), D_MODEL ** -0.5),
        'ffn_conv_w': nrm((L, CONV_W, 2 * D_FF), CONV_W ** -0.5),
        'ffn_conv_b': nrm((L, 2 * D_FF), 0.02),
        'ffn_w_down': nrm((L, D_FF, D_MODEL), D_FF ** -0.5),
        'final_norm_g': 1.0 + nrm((D_MODEL,), 0.02),
    }


def reference(x_prompt, x_sample, state_rwkv, c, c_ctx, ada_w, ada_b, norm_mix_g, w_in,
              rkv_conv_w, rkv_conv_b, decay_up, decay_base, iclr_up, iclr_base, gate_up,
              k_k, k_a, r_k, lnx_g, lnx_b, w_out_a, w_fourier, w_out, norm_ffn_g,
              ffn_w_in, ffn_conv_w, ffn_conv_b, ffn_w_down, final_norm_g):
    f32 = jnp.float32
    ctx_len = x_prompt.shape[1]
    rows = x_sample.shape[1] // GRID_W
    silu_ctx = jax.nn.silu(c_ctx.astype(f32))[None, :]
    silu_c = jax.nn.silu(c.astype(f32))
    xp = x_prompt
    xs = x_sample
    zero_state = jnp.zeros((x_prompt.shape[0], N_DIRS, N_HEADS_A, HEAD_DIM, HEAD_DIM), f32)
    new_states = []
    for l in range(DEPTH):
        lp = (norm_mix_g[l], w_in[l], rkv_conv_w[l], rkv_conv_b[l], decay_up[l], decay_base[l],
              iclr_up[l], iclr_base[l], gate_up[l], k_k[l], k_a[l], r_k[l], lnx_g[l], lnx_b[l],
              w_out_a[l], w_fourier[l], w_out[l], norm_ffn_g[l], ffn_w_in[l], ffn_conv_w[l],
              ffn_conv_b[l], ffn_w_down[l])
        aw = ada_w[l].astype(f32)
        ab = ada_b[l].astype(f32)
        mod_ctx = (silu_ctx @ aw + ab).astype(xp.dtype)
        xp, s_ctx = _layer(xp, mod_ctx, zero_state, 1, ctx_len, lp)
        new_states.append(s_ctx)
        mod_lat = (silu_c @ aw + ab).astype(xs.dtype)
        xs, _ = _layer(xs, mod_lat, state_rwkv[:, l], rows, GRID_W, lp)
    new_state_rwkv = jnp.stack(new_states, axis=1).astype(x_prompt.dtype)
    y_prompt = _rms(xp, final_norm_g)
    y_sample = _rms(xs, final_norm_g)
    return (y_prompt, y_sample, new_state_rwkv)
```

```python
import functools
import math

import numpy as np
import jax
import jax.numpy as jnp
from jax import lax
from jax.experimental import pallas as pl
from jax.experimental.pallas import tpu as pltpu

F32 = jnp.float32
BF16 = jnp.bfloat16

HEAD = 64
LANES = 128
CHUNK = 64
SCAN_TILE = 256
GRID_W = 64
N_GROUPS_B = 4
RMS_EPS = 1e-6
GN_EPS = 64e-5
VMEM_LIMIT = 56 * 1024 * 1024

NN = ((1,), (0,))
NT = ((1,), (1,))
TN = ((0,), (0,))


def _cparams(sem):
    return pltpu.CompilerParams(dimension_semantics=sem, vmem_limit_bytes=VMEM_LIMIT)


def _dot(a, b, dims=NN):
    return lax.dot_general(a, b, (dims, ((), ())), preferred_element_type=F32)


def _split2(x):
    hi = x.astype(BF16)
    lo = (x - hi.astype(F32)).astype(BF16)
    return hi, lo


def _split3(x):
    hi = x.astype(BF16)
    r1 = x - hi.astype(F32)
    mid = r1.astype(BF16)
    lo = (r1 - mid.astype(F32)).astype(BF16)
    return hi, mid, lo


def _dot3(a, b, dims=NN):
    ah, al = _split2(a)
    bh, bl = _split2(b)
    return _dot(ah, bh, dims) + (_dot(ah, bl, dims) + _dot(al, bh, dims))


def _dot_exact_rhs(a, b_bf16, dims=NN):
    ah, al = _split2(a)
    return _dot(ah, b_bf16, dims) + _dot(al, b_bf16, dims)


def _sigmoid(x):
    return 1.0 / (1.0 + jnp.exp(-x))


def _mod_kernel(c_ref, w_ref, b_ref, o_ref):
    c = c_ref[...]
    s = c * _sigmoid(c)
    o_ref[...] = jnp.dot(s, w_ref[...], preferred_element_type=F32,
                         precision=lax.Precision.HIGHEST) + b_ref[...]


def _modulation(cvec, ada_w, ada_b):
    d, n = ada_w.shape
    tn = 1024 if n % 1024 == 0 else 512
    return pl.pallas_call(
        _mod_kernel,
        grid=(n // tn,),
        in_specs=[pl.BlockSpec((8, d), lambda j: (0, 0)),
                  pl.BlockSpec((d, tn), lambda j: (0, j)),
                  pl.BlockSpec((1, tn), lambda j: (0, j))],
        out_specs=pl.BlockSpec((8, tn), lambda j: (0, j)),
        out_shape=jax.ShapeDtypeStruct((8, n), F32),
        compiler_params=_cparams(("arbitrary",)),
        name="modulation",
    )(cvec, ada_w, ada_b.reshape(1, n))


def _rms_mod(x, g, shift, scale):
    ms = jnp.mean(x * x, axis=-1, keepdims=True)
    y = x * lax.rsqrt(ms + RMS_EPS) * g
    return y * (1.0 + scale) + shift


def _inproj_kernel(x_ref, mod_ref, g_ref, w_ref, o_ref, h_ref):
    @pl.when(pl.program_id(1) == 0)
    def _():
        h = _rms_mod(x_ref[...], g_ref[...], mod_ref[0:1, :], mod_ref[1:2, :])
        h_ref[...] = h.astype(BF16)

    o_ref[...] = jnp.dot(h_ref[...], w_ref[...], preferred_element_type=F32)


def _inproj(x, mod, g, w, rows_per_seg, tm, tn):
    r, d = x.shape
    n = w.shape[1]
    tps = rows_per_seg // tm
    return pl.pallas_call(
        _inproj_kernel,
        grid=(r // tm, n // tn),
        in_specs=[pl.BlockSpec((tm, d), lambda i, j: (i, 0)),
                  pl.BlockSpec((None, 6, d), lambda i, j: (i // tps, 0, 0)),
                  pl.BlockSpec((1, d), lambda i, j: (0, 0)),
                  pl.BlockSpec((d, tn), lambda i, j: (0, j))],
        out_specs=pl.BlockSpec((tm, tn), lambda i, j: (i, j)),
        out_shape=jax.ShapeDtypeStruct((r, n), F32),
        scratch_shapes=[pltpu.VMEM((tm, d), BF16)],
        compiler_params=_cparams(("parallel", "arbitrary")),
        name="inproj",
    )(x, mod, g.reshape(1, d), w)


def _conv3(x, w_ref, b_ref, pos, row_len):
    n = x.shape[0]
    prev = jnp.where(pos == 0, 0.0, pltpu.roll(x, 1, 0))
    nxt = jnp.where(pos == row_len - 1, 0.0, pltpu.roll(x, n - 1, 0))
    return prev * w_ref[0:1, :] + x * w_ref[1:2, :] + nxt * w_ref[2:3, :] + b_ref[...]


def _pair_diag(x, lane_lo):
    return jnp.concatenate([jnp.where(lane_lo, x, 0.0), jnp.where(lane_lo, 0.0, x)], axis=0)


def _scan_kernel(*refs, rev, row_len, has_init, emit_bonus, emit_state):
    (rp_ref, kp_ref, vp_ref, dwda_ref, cwr_ref, cwk_ref, cwv_ref, cbr_ref, cbk_ref, cbv_ref,
     decup_ref, decb_ref, iclup_ref, iclb_ref, kkw_ref, kaw_ref, rkw_ref, ones_ref, tri_ref) = refs[:19]
    pos_in = 19
    s0_ref = None
    if has_init:
        s0_ref = refs[pos_in]
        pos_in += 1
    y_ref = refs[pos_in]
    pos_in += 1
    bonus_ref = sfin_ref = None
    if emit_bonus:
        bonus_ref = refs[pos_in]
        pos_in += 1
    if emit_state:
        sfin_ref = refs[pos_in]
        pos_in += 1
    s_ref = refs[pos_in]

    t = pl.program_id(2)
    n_t = pl.num_programs(2)
    tt = rp_ref.shape[0]

    @pl.when(t == 0)
    def _():
        if has_init:
            s_ref[...] = s0_ref[...]
        else:
            s_ref[...] = jnp.zeros_like(s_ref)

    pos = lax.broadcasted_iota(jnp.int32, (tt, LANES), 0) & (row_len - 1)
    r = _conv3(rp_ref[...], cwr_ref, cbr_ref, pos, row_len)
    k = _conv3(kp_ref[...], cwk_ref, cbk_ref, pos, row_len)
    v = _conv3(vp_ref[...], cwv_ref, cbv_ref, pos, row_len)
    ones_bd = ones_ref[...]

    kk = k * kkw_ref[...]
    kk = kk * lax.rsqrt(_dot_exact_rhs(kk * kk, ones_bd) + 1e-12)
    dwda = dwda_ref[...]
    w_logit = decb_ref[...] + _dot3(jnp.tanh(dwda), decup_ref[...])
    logw = (-math.exp(-0.5)) * _sigmoid(w_logit)
    a = _sigmoid(iclb_ref[...] + _dot3(dwda, iclup_ref[...]))
    kd = k * (1.0 + (a - 1.0) * kaw_ref[...])
    bb = kk * a
    if emit_bonus:
        bonus_ref[...] = _dot_exact_rhs(r * k * rkw_ref[...], ones_bd) * v

    tri = tri_ref[...]
    l1, l2, l3 = _split3(logw)
    cum = _dot(tri, l1) + (_dot(tri, l2) + _dot(tri, l3))

    row = lax.broadcasted_iota(jnp.int32, (CHUNK, LANES), 0)
    lane = lax.broadcasted_iota(jnp.int32, (CHUNK, LANES), 1)
    col = lane & (HEAD - 1)
    lane_lo = lane < HEAD
    lane_lo2 = lax.broadcasted_iota(jnp.int32, (2 * CHUNK, LANES), 1) < HEAD
    if rev:
        strict = col > row
        incl = col >= row
    else:
        strict = col < row
        incl = col <= row
    same16 = (row >> 4) == (col >> 4)
    same32 = (row >> 5) == (col >> 5)
    eye = jnp.where(col == row, 1.0, 0.0).astype(F32)

    def mm(p, q):
        return _dot3(p, _pair_diag(q, lane_lo))

    s = s_ref[...]
    n_chunks = tt // CHUNK
    order = range(n_chunks - 1, -1, -1) if rev else range(n_chunks)
    last_row = 0 if rev else CHUNK - 1
    for c in order:
        sl = slice(c * CHUNK, (c + 1) * CHUNK)
        cin = cum[sl]
        lw = logw[sl]
        c_end = cin[last_row:last_row + 1, :]
        g_in = jnp.exp(cin)
        g_ex = jnp.exp(cin - lw)
        g_inv = jnp.exp(-cin)
        g_rat = jnp.exp(c_end - cin)
        g_end = jnp.exp(c_end)
        rt = r[sl] * g_in
        kkt = kk[sl] * g_ex
        kt = kd[sl] * g_inv
        bt = bb[sl] * g_inv
        kh = kd[sl] * g_rat
        bh = bb[sl] * g_rat
        vc = v[sl]

        x = jnp.concatenate([kkt, rt], axis=0)
        sb = _dot3(x, _pair_diag(bt, lane_lo), NT)
        sk = _dot3(x, _pair_diag(kt, lane_lo), NT)
        a_b = jnp.where(strict, sb[:CHUNK], 0.0)
        m_b = jnp.where(incl, sb[CHUNK:], 0.0)
        a_k = jnp.where(strict, sk[:CHUNK], 0.0)
        m_k = jnp.where(incl, sk[CHUNK:], 0.0)

        n0 = jnp.where(same16, a_b, 0.0)
        n2 = mm(n0, n0)
        n4 = mm(n2, n2)
        n8 = mm(n4, n4)
        t1 = eye - n0
        t1 = t1 + mm(t1, n2)
        t1 = t1 + mm(t1, n4)
        t1 = t1 + mm(t1, n8)
        off1 = jnp.where(jnp.logical_and(same32, jnp.logical_not(same16)), a_b, 0.0)
        t2 = t1 - mm(mm(t1, off1), t1)
        off2 = jnp.where(same32, 0.0, a_b)
        tinv = t2 - mm(mm(t2, off2), t2)

        xs = _dot3(x, jnp.concatenate([jnp.where(lane_lo, s, 0.0), jnp.where(lane_lo, 0.0, s)], axis=0), NT)
        u = -mm(tinv, xs[:CHUNK] + mm(a_k, vc))
        y = xs[CHUNK:] + _dot3(jnp.concatenate([m_b, m_k], axis=1),
                               jnp.concatenate([_pair_diag(u, lane_lo), _pair_diag(vc, lane_lo)], axis=0))
        y_ref[sl, :] = y
        full = _dot3(jnp.concatenate([u, vc], axis=0), jnp.concatenate([bh, kh], axis=0), TN)
        s = s * g_end + jnp.where(lane_lo, full[:CHUNK], 0.0) + jnp.where(lane_lo, 0.0, full[CHUNK:])

    s_ref[...] = s
    if emit_state:
        @pl.when(t == n_t - 1)
        def _():
            sfin_ref[...] = s


def _scan(proj, dwda_blk, conv_w, conv_b, dec_up, dec_base, icl_up, icl_base, k_k, k_a, r_k,
          ones_bd, tri, s0, *, batch, seq, d_a, rev, row_len, emit_bonus, emit_state):
    n_t = seq // SCAN_TILE
    n_p = d_a // LANES
    tt = SCAN_TILE
    rows = batch * seq

    def tile(b, t):
        return b * n_t + ((n_t - 1 - t) if rev else t)

    def colspec(base):
        return pl.BlockSpec((tt, LANES), lambda b, p, t, base=base: (tile(b, t), base + p))

    def wspec(nrow, base=0):
        return pl.BlockSpec((nrow, LANES), lambda b, p, t, base=base: (0, base + p))

    const2 = lambda shape: pl.BlockSpec(shape, lambda b, p, t: (0, 0))
    in_specs = [colspec(0), colspec(n_p), colspec(2 * n_p),
                pl.BlockSpec((tt, LANES), lambda b, p, t: (tile(b, t), dwda_blk)),
                wspec(3, 0), wspec(3, n_p), wspec(3, 2 * n_p),
                wspec(1, 0), wspec(1, n_p), wspec(1, 2 * n_p),
                wspec(LANES), wspec(1), wspec(LANES), wspec(1), wspec(1), wspec(1), wspec(1),
                const2((LANES, LANES)), const2((tt, tt))]
    args = [proj, proj, proj, proj, conv_w, conv_w, conv_w, conv_b, conv_b, conv_b,
            dec_up, dec_base, icl_up, icl_base, k_k, k_a, r_k, ones_bd, tri]
    has_init = s0 is not None
    if has_init:
        in_specs.append(pl.BlockSpec((None, None, HEAD, LANES), lambda b, p, t: (b, p, 0, 0)))
        args.append(s0)
    out_specs = [pl.BlockSpec((tt, LANES), lambda b, p, t: (tile(b, t), p))]
    out_shape = [jax.ShapeDtypeStruct((rows, d_a), F32)]
    if emit_bonus:
        out_specs.append(pl.BlockSpec((tt, LANES), lambda b, p, t: (tile(b, t), p)))
        out_shape.append(jax.ShapeDtypeStruct((rows, d_a), F32))
    if emit_state:
        out_specs.append(pl.BlockSpec((None, None, HEAD, LANES), lambda b, p, t: (b, p, 0, 0)))
        out_shape.append(jax.ShapeDtypeStruct((batch, n_p, HEAD, LANES), F32))
    kern = functools.partial(_scan_kernel, rev=rev, row_len=row_len, has_init=has_init,
                             emit_bonus=emit_bonus, emit_state=emit_state)
    return pl.pallas_call(
        kern,
        grid=(batch, n_p, n_t),
        in_specs=in_specs,
        out_specs=out_specs,
        out_shape=out_shape,
        scratch_shapes=[pltpu.VMEM((HEAD, LANES), F32)],
        compiler_params=_cparams(("parallel", "parallel", "arbitrary")),
        name="rwkv7_scan_rev" if rev else "rwkv7_scan_fwd",
    )(*args)


def _chan_dft_kernel(x_ref, w_ref, o_ref):
    o_ref[...] = jnp.dot(x_ref[...].astype(BF16), w_ref[...], preferred_element_type=F32).astype(BF16)


def _chan_dft(proj, w_cs, xb_blk, group, tm):
    r = proj.shape[0]
    return pl.pallas_call(
        _chan_dft_kernel,
        grid=(r // tm, N_GROUPS_B),
        in_specs=[pl.BlockSpec((tm, group), lambda i, g: (i, xb_blk + g)),
                  pl.BlockSpec((group, 2 * group), lambda i, g: (0, 0))],
        out_specs=pl.BlockSpec((tm, 2 * group), lambda i, g: (i, g)),
        out_shape=jax.ShapeDtypeStruct((r, N_GROUPS_B * 2 * group), BF16),
        compiler_params=_cparams(("parallel", "arbitrary")),
        name="fourier_channels",
    )(proj, w_cs)


def _time_dft_kernel(ct_ref, st_ref, z_ref, o_ref, *, group):
    zc = z_ref[:, :group]
    zs = z_ref[:, group:]
    y = jnp.dot(ct_ref[...], zc, preferred_element_type=F32) - jnp.dot(st_ref[...], zs, preferred_element_type=F32)
    o_ref[...] = y.astype(BF16)


def _time_dft(z, ct, st, batch, seq, group, tm):
    n_m = seq // tm
    return pl.pallas_call(
        functools.partial(_time_dft_kernel, group=group),
        grid=(batch, N_GROUPS_B, n_m),
        in_specs=[pl.BlockSpec((tm, seq), lambda b, g, m: (m, 0)),
                  pl.BlockSpec((tm, seq), lambda b, g, m: (m, 0)),
                  pl.BlockSpec((seq, 2 * group), lambda b, g, m: (b, g))],
        out_specs=pl.BlockSpec((tm, group), lambda b, g, m: (b * n_m + m, g)),
        out_shape=jax.ShapeDtypeStruct((batch * seq, N_GROUPS_B * group), BF16),
        compiler_params=_cparams(("parallel", "parallel", "arbitrary")),
        name="fourier_positions",
    )(ct, st, z)


def _dft_mats(n):
    i = lax.broadcasted_iota(jnp.int32, (n, n), 0)
    j = lax.broadcasted_iota(jnp.int32, (n, n), 1)
    ang = ((i * j) % n).astype(F32) * (2.0 * math.pi / n)
    scale = 1.0 / math.sqrt(n)
    return jnp.cos(ang) * scale, jnp.sin(ang) * scale


def _post_kernel(yf_ref, yb_ref, bonus_ref, dg_ref, gup_ref, lg_ref, lb_ref, avg_ref, o_ref):
    y = yf_ref[...] + yb_ref[...]
    avg = avg_ref[...]
    mu = _dot_exact_rhs(y, avg) * (1.0 / HEAD)
    d = y - mu
    var = _dot_exact_rhs(d * d, avg) * (1.0 / HEAD)
    yn = d * lax.rsqrt(var + GN_EPS) * lg_ref[...] + lb_ref[...]
    g = jnp.dot(_sigmoid(dg_ref[...]).astype(BF16), gup_ref[...], preferred_element_type=F32)
    o_ref[...] = ((yn + bonus_ref[...]) * g).astype(BF16)


def _post(yf, yb, bonus, proj, dg_blk, gate_up, lnx_g, lnx_b, avg, tm, cw):
    r, d_a = yf.shape
    lg = gate_up.shape[0]
    tile = pl.BlockSpec((tm, cw), lambda i, j: (i, j))
    vec = pl.BlockSpec((1, cw), lambda i, j: (0, j))
    return pl.pallas_call(
        _post_kernel,
        grid=(r // tm, d_a // cw),
        in_specs=[tile, tile, tile,
                  pl.BlockSpec((tm, lg), lambda i, j: (i, dg_blk)),
                  pl.BlockSpec((lg, cw), lambda i, j: (0, j)),
                  vec, vec,
                  pl.BlockSpec((cw, cw), lambda i, j: (0, 0))],
        out_specs=tile,
        out_shape=jax.ShapeDtypeStruct((r, d_a), BF16),
        compiler_params=_cparams(("parallel", "arbitrary")),
        name="headnorm_gate",
    )(yf, yb, bonus, proj, gate_up, lnx_g.reshape(1, d_a), lnx_b.reshape(1, d_a), avg)


def _merge_kernel(ya_ref, yb_ref, ga_ref, gb_ref, wa_ref, wf_ref, o_ref):
    pa = jnp.dot(ya_ref[...], wa_ref[...], preferred_element_type=F32)
    pb = jnp.dot(yb_ref[...], wf_ref[...], preferred_element_type=F32)
    o_ref[...] = (_sigmoid(ga_ref[...]) * pa + _sigmoid(gb_ref[...]) * pb).astype(BF16)


def _merge(ya, yb, proj, ga_blk, gb_blk, w_a, w_f, tm, tn):
    r, d_a = ya.shape
    d_b = yb.shape[1]
    n = w_a.shape[1]
    return pl.pallas_call(
        _merge_kernel,
        grid=(r // tm, n // tn),
        in_specs=[pl.BlockSpec((tm, d_a), lambda i, j: (i, 0)),
                  pl.BlockSpec((tm, d_b), lambda i, j: (i, 0)),
                  pl.BlockSpec((tm, tn), lambda i, j: (i, ga_blk + j)),
                  pl.BlockSpec((tm, tn), lambda i, j: (i, gb_blk + j)),
                  pl.BlockSpec((d_a, tn), lambda i, j: (0, j)),
                  pl.BlockSpec((d_b, tn), lambda i, j: (0, j))],
        out_specs=pl.BlockSpec((tm, tn), lambda i, j: (i, j)),
        out_shape=jax.ShapeDtypeStruct((r, n), BF16),
        compiler_params=_cparams(("parallel", "arbitrary")),
        name="branch_merge",
    )(ya, yb, proj, proj, w_a, w_f)


def _outproj_kernel(m_ref, w_ref, x_ref, mod_ref, o_ref):
    p = jnp.dot(m_ref[...], w_ref[...], preferred_element_type=F32)
    o_ref[...] = x_ref[...] + mod_ref[2:3, :] * p


def _outproj(mixed, w, x, mod, rows_per_seg, tm, tn):
    r, d = x.shape
    kdim = mixed.shape[1]
    tps = rows_per_seg // tm
    return pl.pallas_call(
        _outproj_kernel,
        grid=(r // tm, d // tn),
        in_specs=[pl.BlockSpec((tm, kdim), lambda i, j: (i, 0)),
                  pl.BlockSpec((kdim, tn), lambda i, j: (0, j)),
                  pl.BlockSpec((tm, tn), lambda i, j: (i, j)),
                  pl.BlockSpec((None, 6, tn), lambda i, j: (i // tps, 0, j))],
        out_specs=pl.BlockSpec((tm, tn), lambda i, j: (i, j)),
        out_shape=jax.ShapeDtypeStruct((r, d), F32),
        compiler_params=_cparams(("parallel", "arbitrary")),
        name="mixer_outproj",
    )(mixed, w, x, mod)


def _ffn_in_kernel(x_ref, mod_ref, g_ref, wg_ref, wv_ref, cwg_ref, cwv_ref, cbg_ref, cbv_ref, o_ref, h_ref,
                   *, row_len):
    @pl.when(pl.program_id(1) == 0)
    def _():
        h = _rms_mod(x_ref[...], g_ref[...], mod_ref[3:4, :], mod_ref[4:5, :])
        h_ref[...] = h.astype(BF16)

    h = h_ref[...]
    tm, tn = o_ref.shape
    pos = lax.broadcasted_iota(jnp.int32, (tm, tn), 0) & (row_len - 1)
    ug = _conv3(jnp.dot(h, wg_ref[...], preferred_element_type=F32), cwg_ref, cbg_ref, pos, row_len)
    uv = _conv3(jnp.dot(h, wv_ref[...], preferred_element_type=F32), cwv_ref, cbv_ref, pos, row_len)
    o_ref[...] = (ug * _sigmoid(ug) * uv).astype(BF16)


def _ffn_in(x, mod, g, w2, cw2, cb2, rows_per_seg, row_len, tm, tn):
    r, d = x.shape
    ffp = w2.shape[1] // 2
    nj = ffp // tn
    tps = rows_per_seg // tm
    return pl.pallas_call(
        functools.partial(_ffn_in_kernel, row_len=row_len),
        grid=(r // tm, nj),
        in_specs=[pl.BlockSpec((tm, d), lambda i, j: (i, 0)),
                  pl.BlockSpec((None, 6, d), lambda i, j: (i // tps, 0, 0)),
                  pl.BlockSpec((1, d), lambda i, j: (0, 0)),
                  pl.BlockSpec((d, tn), lambda i, j: (0, j)),
                  pl.BlockSpec((d, tn), lambda i, j: (0, nj + j)),
                  pl.BlockSpec((3, tn), lambda i, j: (0, j)),
                  pl.BlockSpec((3, tn), lambda i, j: (0, nj + j)),
                  pl.BlockSpec((1, tn), lambda i, j: (0, j)),
                  pl.BlockSpec((1, tn), lambda i, j: (0, nj + j))],
        out_specs=pl.BlockSpec((tm, tn), lambda i, j: (i, j)),
        out_shape=jax.ShapeDtypeStruct((r, ffp), BF16),
        scratch_shapes=[pltpu.VMEM((tm, d), BF16)],
        compiler_params=_cparams(("parallel", "arbitrary")),
        name="ffn_in_conv_gate",
    )(x, mod, g.reshape(1, d), w2, w2, cw2, cw2, cb2, cb2)


def _ffn_down_kernel(a_ref, w_ref, x_ref, mod_ref, g_ref, o_ref, acc_ref):
    kk = pl.program_id(1)

    @pl.when(kk == 0)
    def _():
        acc_ref[...] = jnp.zeros_like(acc_ref)

    acc_ref[...] += jnp.dot(a_ref[...], w_ref[...], preferred_element_type=F32)

    @pl.when(kk == pl.num_programs(1) - 1)
    def _():
        x2 = x_ref[...] + mod_ref[5:6, :] * acc_ref[...]
        ms = jnp.mean(x2 * x2, axis=-1, keepdims=True)
        o_ref[...] = x2 * lax.rsqrt(ms + RMS_EPS) * g_ref[...]


def _ffn_down(act, w, x, mod, g, rows_per_seg, tm, tk):
    r, d = x.shape
    ffp = act.shape[1]
    tps = rows_per_seg // tm
    return pl.pallas_call(
        _ffn_down_kernel,
        grid=(r // tm, ffp // tk),
        in_specs=[pl.BlockSpec((tm, tk), lambda i, k: (i, k)),
                  pl.BlockSpec((tk, d), lambda i, k: (k, 0)),
                  pl.BlockSpec((tm, d), lambda i, k: (i, 0)),
                  pl.BlockSpec((None, 6, d), lambda i, k: (i // tps, 0, 0)),
                  pl.BlockSpec((1, d), lambda i, k: (0, 0))],
        out_specs=pl.BlockSpec((tm, d), lambda i, k: (i, 0)),
        out_shape=jax.ShapeDtypeStruct((r, d), F32),
        scratch_shapes=[pltpu.VMEM((tm, d), F32)],
        compiler_params=_cparams(("parallel", "arbitrary")),
        name="ffn_down_final_norm",
    )(act, w, x, mod, g.reshape(1, d))


def _pick(n, prefs):
    for p in prefs:
        if n % p == 0:
            return p
    raise ValueError(f"no tile in {prefs} divides {n}")


def _pair_layout(s):
    lead = s.shape[:-3]
    h = s.shape[-3]
    s = s.reshape(lead + (h // 2, 2, HEAD, HEAD))
    s = jnp.swapaxes(s, -3, -2)
    return s.reshape(lead + (h // 2, HEAD, 2 * HEAD))


def _unpair_layout(s):
    lead = s.shape[:-3]
    hp = s.shape[-3]
    s = s.reshape(lead + (hp, HEAD, 2, HEAD))
    s = jnp.swapaxes(s, -3, -2)
    return s.reshape(lead + (2 * hp, HEAD, HEAD))


def _layer(x, mod, s0, batch, seq, row_len, wts, emit_state):
    d = x.shape[1]
    rows = batch * seq
    rows_per_seg = rows // mod.shape[0]
    d_a = wts["d_a"]
    d_b = wts["d_b"]
    group = d_b // N_GROUPS_B
    lora_w = wts["lora_w"]
    lora_g = wts["lora_g"]
    c_xb = 3 * d_a
    c_gates = c_xb + d_b
    c_dw = c_gates + 2 * d
    c_dg = c_dw + LANES
    assert lora_w == HEAD and wts["lora_a"] == HEAD and lora_g == LANES
    assert seq % SCAN_TILE == 0 and SCAN_TILE % row_len == 0 and row_len & (row_len - 1) == 0

    tm = _pick(rows_per_seg, (512, 256))
    n_in = wts["w_in"].shape[1]
    proj = _inproj(x, mod, wts["norm_mix_g"], wts["w_in"], rows_per_seg, tm, _pick(n_in, (1280, 768, 512, 256)))

    ys, bonus, states = [], None, []
    for dr in range(2):
        outs = _scan(proj, c_dw // LANES, wts["rkv_conv_w"], wts["rkv_conv_b"],
                     wts["decay_up"][dr], wts["decay_base"][dr], wts["iclr_up"][dr], wts["iclr_base"][dr],
                     wts["k_k"], wts["k_a"], wts["r_k"], wts["ones_bd"], wts["tri"][dr],
                     None if s0 is None else s0[:, dr],
                     batch=batch, seq=seq, d_a=d_a, rev=(dr == 1), row_len=row_len,
                     emit_bonus=(dr == 0), emit_state=emit_state)
        ys.append(outs[0])
        if dr == 0:
            bonus = outs[1]
        if emit_state:
            states.append(outs[-1])

    cw = _pick(d_a, (256,))
    ya = _post(ys[0], ys[1], bonus, proj, c_dg // lora_g, wts["gate_up"], wts["lnx_g"], wts["lnx_b"],
               wts["avg_bd"], _pick(rows, (512, 256)), cw)

    z = _chan_dft(proj, wts["chan_cs"], c_xb // group, group, _pick(rows, (512, 256)))
    ct, st = wts["time_cs"][seq]
    yb = _time_dft(z, ct, st, batch, seq, group, _pick(seq, (512, 256)))

    tn = _pick(d, (512, 256))
    assert c_gates % tn == 0 and d % tn == 0
    mixed = _merge(ya, yb, proj, c_gates // tn, (c_gates + d) // tn, wts["w_out_a"], wts["w_fourier"], tm, tn)
    x1 = _outproj(mixed, wts["w_out"], x, mod, rows_per_seg, tm, tn)

    act = _ffn_in(x1, mod, wts["norm_ffn_g"], wts["ffn_w_in"], wts["ffn_conv_w"], wts["ffn_conv_b"],
                  rows_per_seg, row_len, tm, 512)
    y = _ffn_down(act, wts["ffn_w_down"], x1, mod, wts["final_norm_g"], rows_per_seg, tm, 512)
    return y, states


def _tri_blockdiag(n, rev):
    i = np.arange(n)[:, None]
    j = np.arange(n)[None, :]
    same = (i // CHUNK) == (j // CHUNK)
    m = same & ((j >= i) if rev else (j <= i))
    return jnp.asarray(m.astype(np.float32), dtype=BF16)


def _blockdiag_ones(n):
    i = np.arange(n)[:, None] // HEAD
    j = np.arange(n)[None, :] // HEAD
    return jnp.asarray((i == j).astype(np.float32), dtype=BF16)


def kernel(x_prompt, x_sample, state_rwkv, c, c_ctx, ada_w, ada_b, norm_mix_g, w_in, rkv_conv_w, rkv_conv_b,
           decay_up, decay_base, iclr_up, iclr_base, gate_up, k_k, k_a, r_k, lnx_g, lnx_b, w_out_a, w_fourier,
           w_out, norm_ffn_g, ffn_w_in, ffn_conv_w, ffn_conv_b, ffn_w_down, final_norm_g):
    batch, ctx_len, d = x_prompt.shape
    dec_batch, dec_seq, _ = x_sample.shape
    depth = ada_w.shape[0]
    assert depth == 1
    d_a = w_out_a.shape[1]
    d_b = w_fourier.shape[1]
    n_heads = d_a // HEAD
    d_ff = ffn_w_down.shape[1]
    ffp = -(-d_ff // 512) * 512
    lora_w = decay_up.shape[2]
    lora_a = iclr_up.shape[2]
    lora_g = gate_up.shape[1]
    group = d_b // N_GROUPS_B

    cvec = jnp.concatenate([c_ctx[None, :], c], axis=0).astype(F32)
    n_vec = cvec.shape[0]
    cvec = jnp.pad(cvec, ((0, 8 - n_vec), (0, 0)))
    mod = _modulation(cvec, ada_w[0].astype(F32), ada_b[0].astype(F32))[:n_vec].reshape(n_vec, 6, d)

    l = 0
    zpad_w = jnp.zeros((2, LANES - lora_w, d_a), F32)
    zpad_a = jnp.zeros((2, LANES - lora_a, d_a), F32)
    ffn_in = ffn_w_in[l]
    pad_ff = ((0, 0), (0, ffp - d_ff))
    cc, sc = _dft_mats(group)
    wts = {
        "d_a": d_a, "d_b": d_b, "lora_w": lora_w, "lora_a": lora_a, "lora_g": lora_g,
        "norm_mix_g": norm_mix_g[l], "w_in": w_in[l].astype(BF16),
        "rkv_conv_w": rkv_conv_w[l], "rkv_conv_b": rkv_conv_b[l].reshape(1, -1),
        "decay_up": jnp.concatenate([decay_up[l], zpad_w], axis=1),
        "iclr_up": jnp.concatenate([zpad_a, iclr_up[l]], axis=1),
        "decay_base": decay_base[l].reshape(2, 1, d_a), "iclr_base": iclr_base[l].reshape(2, 1, d_a),
        "k_k": k_k[l].reshape(1, d_a), "k_a": k_a[l].reshape(1, d_a), "r_k": r_k[l].reshape(1, d_a),
        "gate_up": gate_up[l].astype(BF16), "lnx_g": lnx_g[l], "lnx_b": lnx_b[l],
        "w_out_a": w_out_a[l].astype(BF16), "w_fourier": w_fourier[l].astype(BF16), "w_out": w_out[l].astype(BF16),
        "norm_ffn_g": norm_ffn_g[l],
        "ffn_w_in": jnp.concatenate([jnp.pad(ffn_in[:, :d_ff], pad_ff), jnp.pad(ffn_in[:, d_ff:], pad_ff)],
                                    axis=1).astype(BF16),
        "ffn_conv_w": jnp.concatenate([jnp.pad(ffn_conv_w[l][:, :d_ff], pad_ff),
                                       jnp.pad(ffn_conv_w[l][:, d_ff:], pad_ff)], axis=1),
        "ffn_conv_b": jnp.concatenate([jnp.pad(ffn_conv_b[l][None, :d_ff], pad_ff),
                                       jnp.pad(ffn_conv_b[l][None, d_ff:], pad_ff)], axis=1),
        "ffn_w_down": jnp.pad(ffn_w_down[l], ((0, ffp - d_ff), (0, 0))).astype(BF16),
        "final_norm_g": final_norm_g,
        "ones_bd": _blockdiag_ones(LANES), "avg_bd": _blockdiag_ones(256),
        "tri": (_tri_blockdiag(SCAN_TILE, False), _tri_blockdiag(SCAN_TILE, True)),
        "chan_cs": jnp.concatenate([cc, sc], axis=1).astype(BF16),
        "time_cs": {n: tuple(m.astype(BF16) for m in _dft_mats(n)) for n in {ctx_len, dec_seq}},
    }

    y_ctx, st_ctx = _layer(x_prompt.reshape(batch * ctx_len, d), mod[:1], None, batch, ctx_len, ctx_len,
                           wts, emit_state=True)
    s0 = _pair_layout(state_rwkv[:, l].astype(F32))
    y_lat, _ = _layer(x_sample.reshape(dec_batch * dec_seq, d), mod[1:], s0, dec_batch, dec_seq, GRID_W,
                      wts, emit_state=False)

    new_state = jnp.stack([_unpair_layout(st_ctx[0]), _unpair_layout(st_ctx[1])], axis=1)
    new_state = new_state[:, None].astype(x_prompt.dtype)
    return (y_ctx.reshape(batch, ctx_len, d), y_lat.reshape(dec_batch, dec_seq, d), new_state)
```

```python
import functools
import math

import numpy as np
import jax
import jax.numpy as jnp
from jax import lax
from jax.experimental import pallas as pl
from jax.experimental.pallas import tpu as pltpu

F32 = jnp.float32
BF16 = jnp.bfloat16

HEAD = 64
LANES = 128
CHUNK = 64
SCAN_TILE = 256
SCAN_PAIRS = 8
GRID_W = 64
N_GROUPS_B = 4
RMS_EPS = 1e-6
GN_EPS = 64e-5
VMEM_LIMIT = 56 * 1024 * 1024

NN = ((1,), (0,))
NT = ((1,), (1,))
TN = ((0,), (0,))


def _cparams(sem):
    return pltpu.CompilerParams(dimension_semantics=sem, vmem_limit_bytes=VMEM_LIMIT)


def _dot(a, b, dims=NN):
    return lax.dot_general(a, b, (dims, ((), ())), preferred_element_type=F32)


def _split2(x):
    hi = x.astype(BF16)
    lo = (x - hi.astype(F32)).astype(BF16)
    return hi, lo


def _split3(x):
    hi = x.astype(BF16)
    r1 = x - hi.astype(F32)
    mid = r1.astype(BF16)
    lo = (r1 - mid.astype(F32)).astype(BF16)
    return hi, mid, lo


def _dot3(a, b, dims=NN):
    ah, al = _split2(a)
    bh, bl = _split2(b)
    return _dot(ah, bh, dims) + (_dot(ah, bl, dims) + _dot(al, bh, dims))


def _dot_exact_rhs(a, b_bf16, dims=NN):
    ah, al = _split2(a)
    return _dot(ah, b_bf16, dims) + _dot(al, b_bf16, dims)


def _sigmoid(x):
    return 1.0 / (1.0 + jnp.exp(-x))


def _mod_kernel(c_ref, w_ref, b_ref, o_ref):
    c = c_ref[...]
    s = c * _sigmoid(c)
    o_ref[...] = jnp.dot(s, w_ref[...], preferred_element_type=F32,
                         precision=lax.Precision.HIGHEST) + b_ref[...]


def _modulation(cvec, ada_w, ada_b):
    d, n = ada_w.shape
    tn = 1024 if n % 1024 == 0 else 512
    return pl.pallas_call(
        _mod_kernel,
        grid=(n // tn,),
        in_specs=[pl.BlockSpec((8, d), lambda j: (0, 0)),
                  pl.BlockSpec((d, tn), lambda j: (0, j)),
                  pl.BlockSpec((1, tn), lambda j: (0, j))],
        out_specs=pl.BlockSpec((8, tn), lambda j: (0, j)),
        out_shape=jax.ShapeDtypeStruct((8, n), F32),
        compiler_params=_cparams(("arbitrary",)),
        name="modulation",
    )(cvec, ada_w, ada_b.reshape(1, n))


def _rms_mod(x, g, shift, scale):
    ms = jnp.mean(x * x, axis=-1, keepdims=True)
    y = x * lax.rsqrt(ms + RMS_EPS) * g
    return y * (1.0 + scale) + shift


def _inproj_kernel(x_ref, mod_ref, g_ref, w_ref, o_ref, h_ref):
    @pl.when(pl.program_id(1) == 0)
    def _():
        h = _rms_mod(x_ref[...], g_ref[...], mod_ref[0:1, :], mod_ref[1:2, :])
        h_ref[...] = h.astype(BF16)

    o_ref[...] = jnp.dot(h_ref[...], w_ref[...], preferred_element_type=F32)


def _inproj(x, mod, g, w, rows_per_seg, tm, tn):
    r, d = x.shape
    n = w.shape[1]
    tps = rows_per_seg // tm
    return pl.pallas_call(
        _inproj_kernel,
        grid=(r // tm, n // tn),
        in_specs=[pl.BlockSpec((tm, d), lambda i, j: (i, 0)),
                  pl.BlockSpec((None, 6, d), lambda i, j: (i // tps, 0, 0)),
                  pl.BlockSpec((1, d), lambda i, j: (0, 0)),
                  pl.BlockSpec((d, tn), lambda i, j: (0, j))],
        out_specs=pl.BlockSpec((tm, tn), lambda i, j: (i, j)),
        out_shape=jax.ShapeDtypeStruct((r, n), F32),
        scratch_shapes=[pltpu.VMEM((tm, d), BF16)],
        compiler_params=_cparams(("parallel", "arbitrary")),
        name="inproj",
    )(x, mod, g.reshape(1, d), w)


def _conv3(x, w_ref, b_ref, pos, row_len):
    n = x.shape[0]
    prev = jnp.where(pos == 0, 0.0, pltpu.roll(x, 1, 0))
    nxt = jnp.where(pos == row_len - 1, 0.0, pltpu.roll(x, n - 1, 0))
    return prev * w_ref[0:1, :] + x * w_ref[1:2, :] + nxt * w_ref[2:3, :] + b_ref[...]


def _pair_diag(x, lane_lo):
    return jnp.concatenate([jnp.where(lane_lo, x, 0.0), jnp.where(lane_lo, 0.0, x)], axis=0)


def _scan_kernel(*refs, rev, row_len, has_init, emit_bonus, emit_state, n_pairs):
    (rp_ref, kp_ref, vp_ref, dwda_ref, cwr_ref, cwk_ref, cwv_ref, cbr_ref, cbk_ref, cbv_ref,
     decup_ref, decb_ref, iclup_ref, iclb_ref, kkw_ref, kaw_ref, rkw_ref, ones_ref, tri_ref) = refs[:19]
    rest = list(refs[19:])
    s0_ref = rest.pop(0) if has_init else None
    y_ref = rest.pop(0)
    bonus_ref = rest.pop(0) if emit_bonus else None
    sfin_ref = rest.pop(0) if emit_state else None
    s_ref = rest.pop(0)

    t = pl.program_id(2)
    n_t = pl.num_programs(2)
    tt, width = rp_ref.shape

    @pl.when(t == 0)
    def _():
        if has_init:
            s_ref[...] = s0_ref[...]
        else:
            s_ref[...] = jnp.zeros_like(s_ref)

    ones_bd = ones_ref[...]

    def head_sum(x):
        return jnp.concatenate([_dot_exact_rhs(x[:, p * LANES:(p + 1) * LANES], ones_bd) for p in range(n_pairs)],
                               axis=1)

    pos = lax.broadcasted_iota(jnp.int32, (tt, width), 0) & (row_len - 1)
    r = _conv3(rp_ref[...], cwr_ref, cbr_ref, pos, row_len)
    k = _conv3(kp_ref[...], cwk_ref, cbk_ref, pos, row_len)
    v = _conv3(vp_ref[...], cwv_ref, cbv_ref, pos, row_len)

    kk = k * kkw_ref[...]
    kk = kk * lax.rsqrt(head_sum(kk * kk) + 1e-12)
    dwda = dwda_ref[...]
    w_logit = decb_ref[...] + _dot3(jnp.tanh(dwda), decup_ref[...])
    logw = (-math.exp(-0.5)) * _sigmoid(w_logit)
    a = _sigmoid(iclb_ref[...] + _dot3(dwda, iclup_ref[...]))
    kd = k * (1.0 + (a - 1.0) * kaw_ref[...])
    bb = kk * a
    if emit_bonus:
        bonus_ref[...] = head_sum(r * k * rkw_ref[...]) * v

    tri = tri_ref[...]
    l1, l2, l3 = _split3(logw)
    cum = _dot(tri, l1) + (_dot(tri, l2) + _dot(tri, l3))

    row = lax.broadcasted_iota(jnp.int32, (CHUNK, LANES), 0)
    lane = lax.broadcasted_iota(jnp.int32, (CHUNK, LANES), 1)
    col = lane & (HEAD - 1)
    lane_lo = lane < HEAD
    if rev:
        strict = col > row
        incl = col >= row
    else:
        strict = col < row
        incl = col <= row
    same16 = (row >> 4) == (col >> 4)
    same32 = (row >> 5) == (col >> 5)
    off16 = jnp.logical_and(same32, jnp.logical_not(same16))
    eye = jnp.where(col == row, 1.0, 0.0).astype(F32)
    eye_t = (lax.broadcasted_iota(jnp.int32, (LANES, LANES), 0)
             == lax.broadcasted_iota(jnp.int32, (LANES, LANES), 1)).astype(BF16)

    def b16(x):
        return x.astype(BF16)

    def pd(x):
        return b16(_pair_diag(x, lane_lo))

    def mm(p, q_pd):
        return _dot(b16(p), q_pd)

    n_chunks = tt // CHUNK
    order = list(range(n_chunks - 1, -1, -1) if rev else range(n_chunks))
    last_row = 0 if rev else CHUNK - 1
    streams = [(c, p) for c in order for p in range(n_pairs)]

    q = {}
    for c in order:
        sl = slice(c * CHUNK, (c + 1) * CHUNK)
        cin = cum[sl]
        c_end = cin[last_row:last_row + 1, :]
        g_inv = jnp.exp(-cin)
        g_rat = jnp.exp(c_end - cin)
        rt = r[sl] * jnp.exp(cin)
        kkt = kk[sl] * jnp.exp(cin - logw[sl])
        kt = kd[sl] * g_inv
        bt = bb[sl] * g_inv
        kh = kd[sl] * g_rat
        bh = bb[sl] * g_rat
        vc = v[sl]
        for p in range(n_pairs):
            ls = slice(p * LANES, (p + 1) * LANES)
            q[c, p] = dict(sl=sl, ls=ls, c_end=c_end[:, ls], rt=b16(rt[:, ls]), kkt=kkt[:, ls], kt=kt[:, ls],
                           bt=bt[:, ls], bkh=b16(jnp.concatenate([bh[:, ls], kh[:, ls]], axis=0)), vc=vc[:, ls])
    for st in streams:
        d = q[st]
        x = jnp.concatenate([b16(d["kkt"]), d["rt"]], axis=0)
        sb = _dot(x, pd(d["bt"]), NT)
        sk = _dot(x, pd(d["kt"]), NT)
        d["a_b"] = jnp.where(strict, sb[:CHUNK], 0.0)
        d["a_k"] = jnp.where(strict, sk[:CHUNK], 0.0)
        d["m_bk"] = b16(jnp.concatenate([jnp.where(incl, sb[CHUNK:], 0.0), jnp.where(incl, sk[CHUNK:], 0.0)], axis=1))
        d["n0"] = jnp.where(same16, d["a_b"], 0.0)
        d["vpd"] = pd(d["vc"])
    for st in streams:
        d = q[st]
        d["n2"] = mm(d["n0"], pd(d["n0"]))
        d["akv"] = mm(d["a_k"], d["vpd"])
    for st in streams:
        d = q[st]
        t1 = eye - d["n0"]
        both = _dot(b16(jnp.concatenate([d["n2"], t1], axis=0)), pd(d["n2"]))
        d["n4"] = both[:CHUNK]
        d["t"] = t1 + both[CHUNK:]
    for st in streams:
        d = q[st]
        both = _dot(b16(jnp.concatenate([d["n4"], d["t"]], axis=0)), pd(d["n4"]))
        d["n8"] = both[:CHUNK]
        d["t"] = d["t"] + both[CHUNK:]
    for st in streams:
        d = q[st]
        d["t"] = d["t"] + mm(d["t"], pd(d["n8"]))
    for st in streams:
        d = q[st]
        d["p"] = mm(d["t"], pd(jnp.where(off16, d["a_b"], 0.0)))
    for st in streams:
        d = q[st]
        d["t"] = d["t"] - mm(d["p"], pd(d["t"]))
    for st in streams:
        d = q[st]
        d["p"] = mm(d["t"], pd(jnp.where(same32, 0.0, d["a_b"])))
    for st in streams:
        d = q[st]
        d["t"] = d["t"] - mm(d["p"], pd(d["t"]))
    for st in streams:
        d = q[st]
        w12 = _dot(b16(d["t"]), jnp.concatenate([pd(d["kkt"]), pd(d["akv"])], axis=1))
        d["w1r"] = jnp.concatenate([b16(w12[:, :LANES]), d["rt"]], axis=0)
        d["w2"] = w12[:, LANES:]
        d["bk_t"] = b16(_dot(eye_t, d["bkh"], NT))
        c1, c2, c3 = _split3(jnp.where(col == row, d["c_end"], 0.0))
        d["g_col"] = jnp.exp(_dot(c1, ones_bd) + (_dot(c2, ones_bd) + _dot(c3, ones_bd)))

    s = [s_ref[p] for p in range(n_pairs)]
    for c in order:
        for p in range(n_pairs):
            d = q[c, p]
            xs = _dot(d["w1r"], pd(s[p]))
            d["u"] = -(xs[:CHUNK] + d["w2"])
            d["ys"] = xs[CHUNK:]
            full = _dot(d["bk_t"], b16(jnp.concatenate([d["u"], d["vc"]], axis=0)))
            s[p] = s[p] * d["g_col"] + jnp.where(lane_lo, full[:CHUNK], full[CHUNK:])
    for st in streams:
        d = q[st]
        y_ref[d["sl"], d["ls"]] = d["ys"] + _dot(d["m_bk"], jnp.concatenate([pd(d["u"]), d["vpd"]], axis=0))

    for p in range(n_pairs):
        s_ref[p] = s[p]
    if emit_state:
        @pl.when(t == n_t - 1)
        def _():
            for p in range(n_pairs):
                sfin_ref[p] = s[p]


def _scan(proj, dwda_blk, conv_w, conv_b, dec_up, dec_base, icl_up, icl_base, k_k, k_a, r_k,
          ones_bd, tri, s0, *, batch, seq, d_a, rev, row_len, emit_bonus, emit_state):
    n_t = seq // SCAN_TILE
    n_p = d_a // LANES
    pps = SCAN_PAIRS
    assert n_p % pps == 0
    n_g = n_p // pps
    width = pps * LANES
    tt = SCAN_TILE
    rows = batch * seq

    def tile(b, t):
        return b * n_t + ((n_t - 1 - t) if rev else t)

    def colspec(base):
        return pl.BlockSpec((tt, width), lambda b, g, t, base=base: (tile(b, t), base + g))

    def wspec(nrow, base=0):
        return pl.BlockSpec((nrow, width), lambda b, g, t, base=base: (0, base + g))

    const2 = lambda shape: pl.BlockSpec(shape, lambda b, g, t: (0, 0))
    state_spec = pl.BlockSpec((None, pps, HEAD, LANES), lambda b, g, t: (b, g, 0, 0))
    in_specs = [colspec(0), colspec(n_g), colspec(2 * n_g),
                pl.BlockSpec((tt, LANES), lambda b, g, t: (tile(b, t), dwda_blk)),
                wspec(3, 0), wspec(3, n_g), wspec(3, 2 * n_g),
                wspec(1, 0), wspec(1, n_g), wspec(1, 2 * n_g),
                wspec(LANES), wspec(1), wspec(LANES), wspec(1), wspec(1), wspec(1), wspec(1),
                const2((LANES, LANES)), const2((tt, tt))]
    args = [proj, proj, proj, proj, conv_w, conv_w, conv_w, conv_b, conv_b, conv_b,
            dec_up, dec_base, icl_up, icl_base, k_k, k_a, r_k, ones_bd, tri]
    has_init = s0 is not None
    if has_init:
        in_specs.append(state_spec)
        args.append(s0)
    out_specs = [pl.BlockSpec((tt, width), lambda b, g, t: (tile(b, t), g))]
    out_shape = [jax.ShapeDtypeStruct((rows, d_a), F32)]
    if emit_bonus:
        out_specs.append(pl.BlockSpec((tt, width), lambda b, g, t: (tile(b, t), g)))
        out_shape.append(jax.ShapeDtypeStruct((rows, d_a), F32))
    if emit_state:
        out_specs.append(state_spec)
        out_shape.append(jax.ShapeDtypeStruct((batch, n_p, HEAD, LANES), F32))
    kern = functools.partial(_scan_kernel, rev=rev, row_len=row_len, has_init=has_init,
                             emit_bonus=emit_bonus, emit_state=emit_state, n_pairs=pps)
    return pl.pallas_call(
        kern,
        grid=(batch, n_g, n_t),
        in_specs=in_specs,
        out_specs=out_specs,
        out_shape=out_shape,
        scratch_shapes=[pltpu.VMEM((pps, HEAD, LANES), F32)],
        compiler_params=_cparams(("parallel", "parallel", "arbitrary")),
        name="rwkv7_scan_rev" if rev else "rwkv7_scan_fwd",
    )(*args)


def _chan_dft_kernel(x_ref, w_ref, o_ref):
    o_ref[...] = jnp.dot(x_ref[...].astype(BF16), w_ref[...], preferred_element_type=F32).astype(BF16)


def _chan_dft(proj, w_cs, xb_blk, group, tm):
    r = proj.shape[0]
    return pl.pallas_call(
        _chan_dft_kernel,
        grid=(r // tm, N_GROUPS_B),
        in_specs=[pl.BlockSpec((tm, group), lambda i, g: (i, xb_blk + g)),
                  pl.BlockSpec((group, 2 * group), lambda i, g: (0, 0))],
        out_specs=pl.BlockSpec((tm, 2 * group), lambda i, g: (i, g)),
        out_shape=jax.ShapeDtypeStruct((r, N_GROUPS_B * 2 * group), BF16),
        compiler_params=_cparams(("parallel", "arbitrary")),
        name="fourier_channels",
    )(proj, w_cs)


def _time_dft_kernel(ct_ref, st_ref, z_ref, o_ref, *, group):
    zc = z_ref[:, :group]
    zs = z_ref[:, group:]
    y = jnp.dot(ct_ref[...], zc, preferred_element_type=F32) - jnp.dot(st_ref[...], zs, preferred_element_type=F32)
    o_ref[...] = y.astype(BF16)


def _time_dft(z, ct, st, batch, seq, group, tm):
    n_m = seq // tm
    return pl.pallas_call(
        functools.partial(_time_dft_kernel, group=group),
        grid=(batch, N_GROUPS_B, n_m),
        in_specs=[pl.BlockSpec((tm, seq), lambda b, g, m: (m, 0)),
                  pl.BlockSpec((tm, seq), lambda b, g, m: (m, 0)),
                  pl.BlockSpec((seq, 2 * group), lambda b, g, m: (b, g))],
        out_specs=pl.BlockSpec((tm, group), lambda b, g, m: (b * n_m + m, g)),
        out_shape=jax.ShapeDtypeStruct((batch * seq, N_GROUPS_B * group), BF16),
        compiler_params=_cparams(("parallel", "parallel", "arbitrary")),
        name="fourier_positions",
    )(ct, st, z)


def _dft_mats(n):
    i = lax.broadcasted_iota(jnp.int32, (n, n), 0)
    j = lax.broadcasted_iota(jnp.int32, (n, n), 1)
    ang = ((i * j) % n).astype(F32) * (2.0 * math.pi / n)
    scale = 1.0 / math.sqrt(n)
    return jnp.cos(ang) * scale, jnp.sin(ang) * scale


def _post_kernel(yf_ref, yb_ref, bonus_ref, dg_ref, gup_ref, lg_ref, lb_ref, avg_ref, o_ref):
    y = yf_ref[...] + yb_ref[...]
    avg = avg_ref[...]
    mu = _dot_exact_rhs(y, avg) * (1.0 / HEAD)
    d = y - mu
    var = _dot_exact_rhs(d * d, avg) * (1.0 / HEAD)
    yn = d * lax.rsqrt(var + GN_EPS) * lg_ref[...] + lb_ref[...]
    g = jnp.dot(_sigmoid(dg_ref[...]).astype(BF16), gup_ref[...], preferred_element_type=F32)
    o_ref[...] = ((yn + bonus_ref[...]) * g).astype(BF16)


def _post(yf, yb, bonus, proj, dg_blk, gate_up, lnx_g, lnx_b, avg, tm, cw):
    r, d_a = yf.shape
    lg = gate_up.shape[0]
    tile = pl.BlockSpec((tm, cw), lambda i, j: (i, j))
    vec = pl.BlockSpec((1, cw), lambda i, j: (0, j))
    return pl.pallas_call(
        _post_kernel,
        grid=(r // tm, d_a // cw),
        in_specs=[tile, tile, tile,
                  pl.BlockSpec((tm, lg), lambda i, j: (i, dg_blk)),
                  pl.BlockSpec((lg, cw), lambda i, j: (0, j)),
                  vec, vec,
                  pl.BlockSpec((cw, cw), lambda i, j: (0, 0))],
        out_specs=tile,
        out_shape=jax.ShapeDtypeStruct((r, d_a), BF16),
        compiler_params=_cparams(("parallel", "arbitrary")),
        name="headnorm_gate",
    )(yf, yb, bonus, proj, gate_up, lnx_g.reshape(1, d_a), lnx_b.reshape(1, d_a), avg)


def _merge_kernel(ya_ref, yb_ref, ga_ref, gb_ref, wa_ref, wf_ref, o_ref):
    pa = jnp.dot(ya_ref[...], wa_ref[...], preferred_element_type=F32)
    pb = jnp.dot(yb_ref[...], wf_ref[...], preferred_element_type=F32)
    o_ref[...] = (_sigmoid(ga_ref[...]) * pa + _sigmoid(gb_ref[...]) * pb).astype(BF16)


def _merge(ya, yb, proj, ga_blk, gb_blk, w_a, w_f, tm, tn):
    r, d_a = ya.shape
    d_b = yb.shape[1]
    n = w_a.shape[1]
    return pl.pallas_call(
        _merge_kernel,
        grid=(r // tm, n // tn),
        in_specs=[pl.BlockSpec((tm, d_a), lambda i, j: (i, 0)),
                  pl.BlockSpec((tm, d_b), lambda i, j: (i, 0)),
                  pl.BlockSpec((tm, tn), lambda i, j: (i, ga_blk + j)),
                  pl.BlockSpec((tm, tn), lambda i, j: (i, gb_blk + j)),
                  pl.BlockSpec((d_a, tn), lambda i, j: (0, j)),
                  pl.BlockSpec((d_b, tn), lambda i, j: (0, j))],
        out_specs=pl.BlockSpec((tm, tn), lambda i, j: (i, j)),
        out_shape=jax.ShapeDtypeStruct((r, n), BF16),
        compiler_params=_cparams(("parallel", "arbitrary")),
        name="branch_merge",
    )(ya, yb, proj, proj, w_a, w_f)


def _outproj_kernel(m_ref, w_ref, x_ref, mod_ref, o_ref):
    p = jnp.dot(m_ref[...], w_ref[...], preferred_element_type=F32)
    o_ref[...] = x_ref[...] + mod_ref[2:3, :] * p


def _outproj(mixed, w, x, mod, rows_per_seg, tm, tn):
    r, d = x.shape
    kdim = mixed.shape[1]
    tps = rows_per_seg // tm
    return pl.pallas_call(
        _outproj_kernel,
        grid=(r // tm, d // tn),
        in_specs=[pl.BlockSpec((tm, kdim), lambda i, j: (i, 0)),
                  pl.BlockSpec((kdim, tn), lambda i, j: (0, j)),
                  pl.BlockSpec((tm, tn), lambda i, j: (i, j)),
                  pl.BlockSpec((None, 6, tn), lambda i, j: (i // tps, 0, j))],
        out_specs=pl.BlockSpec((tm, tn), lambda i, j: (i, j)),
        out_shape=jax.ShapeDtypeStruct((r, d), F32),
        compiler_params=_cparams(("parallel", "arbitrary")),
        name="mixer_outproj",
    )(mixed, w, x, mod)


def _ffn_in_kernel(x_ref, mod_ref, g_ref, wg_ref, wv_ref, cwg_ref, cwv_ref, cbg_ref, cbv_ref, o_ref, h_ref,
                   *, row_len):
    @pl.when(pl.program_id(1) == 0)
    def _():
        h = _rms_mod(x_ref[...], g_ref[...], mod_ref[3:4, :], mod_ref[4:5, :])
        h_ref[...] = h.astype(BF16)

    h = h_ref[...]
    tm, tn = o_ref.shape
    pos = lax.broadcasted_iota(jnp.int32, (tm, tn), 0) & (row_len - 1)
    ug = _conv3(jnp.dot(h, wg_ref[...], preferred_element_type=F32), cwg_ref, cbg_ref, pos, row_len)
    uv = _conv3(jnp.dot(h, wv_ref[...], preferred_element_type=F32), cwv_ref, cbv_ref, pos, row_len)
    o_ref[...] = (ug * _sigmoid(ug) * uv).astype(BF16)


def _ffn_in(x, mod, g, w2, cw2, cb2, rows_per_seg, row_len, tm, tn):
    r, d = x.shape
    ffp = w2.shape[1] // 2
    nj = ffp // tn
    tps = rows_per_seg // tm
    return pl.pallas_call(
        functools.partial(_ffn_in_kernel, row_len=row_len),
        grid=(r // tm, nj),
        in_specs=[pl.BlockSpec((tm, d), lambda i, j: (i, 0)),
                  pl.BlockSpec((None, 6, d), lambda i, j: (i // tps, 0, 0)),
                  pl.BlockSpec((1, d), lambda i, j: (0, 0)),
                  pl.BlockSpec((d, tn), lambda i, j: (0, j)),
                  pl.BlockSpec((d, tn), lambda i, j: (0, nj + j)),
                  pl.BlockSpec((3, tn), lambda i, j: (0, j)),
                  pl.BlockSpec((3, tn), lambda i, j: (0, nj + j)),
                  pl.BlockSpec((1, tn), lambda i, j: (0, j)),
                  pl.BlockSpec((1, tn), lambda i, j: (0, nj + j))],
        out_specs=pl.BlockSpec((tm, tn), lambda i, j: (i, j)),
        out_shape=jax.ShapeDtypeStruct((r, ffp), BF16),
        scratch_shapes=[pltpu.VMEM((tm, d), BF16)],
        compiler_params=_cparams(("parallel", "arbitrary")),
        name="ffn_in_conv_gate",
    )(x, mod, g.reshape(1, d), w2, w2, cw2, cw2, cb2, cb2)


def _ffn_down_kernel(a_ref, w_ref, x_ref, mod_ref, g_ref, o_ref, acc_ref):
    kk = pl.program_id(1)

    @pl.when(kk == 0)
    def _():
        acc_ref[...] = jnp.zeros_like(acc_ref)

    acc_ref[...] += jnp.dot(a_ref[...], w_ref[...], preferred_element_type=F32)

    @pl.when(kk == pl.num_programs(1) - 1)
    def _():
        x2 = x_ref[...] + mod_ref[5:6, :] * acc_ref[...]
        ms = jnp.mean(x2 * x2, axis=-1, keepdims=True)
        o_ref[...] = x2 * lax.rsqrt(ms + RMS_EPS) * g_ref[...]


def _ffn_down(act, w, x, mod, g, rows_per_seg, tm, tk):
    r, d = x.shape
    ffp = act.shape[1]
    tps = rows_per_seg // tm
    return pl.pallas_call(
        _ffn_down_kernel,
        grid=(r // tm, ffp // tk),
        in_specs=[pl.BlockSpec((tm, tk), lambda i, k: (i, k)),
                  pl.BlockSpec((tk, d), lambda i, k: (k, 0)),
                  pl.BlockSpec((tm, d), lambda i, k: (i, 0)),
                  pl.BlockSpec((None, 6, d), lambda i, k: (i // tps, 0, 0)),
                  pl.BlockSpec((1, d), lambda i, k: (0, 0))],
        out_specs=pl.BlockSpec((tm, d), lambda i, k: (i, 0)),
        out_shape=jax.ShapeDtypeStruct((r, d), F32),
        scratch_shapes=[pltpu.VMEM((tm, d), F32)],
        compiler_params=_cparams(("parallel", "arbitrary")),
        name="ffn_down_final_norm",
    )(act, w, x, mod, g.reshape(1, d))


def _pick(n, prefs):
    for p in prefs:
        if n % p == 0:
            return p
    raise ValueError(f"no tile in {prefs} divides {n}")


def _pair_layout(s):
    lead = s.shape[:-3]
    h = s.shape[-3]
    s = s.reshape(lead + (h // 2, 2, HEAD, HEAD))
    s = jnp.swapaxes(s, -3, -2)
    return s.reshape(lead + (h // 2, HEAD, 2 * HEAD))


def _unpair_layout(s):
    lead = s.shape[:-3]
    hp = s.shape[-3]
    s = s.reshape(lead + (hp, HEAD, 2, HEAD))
    s = jnp.swapaxes(s, -3, -2)
    return s.reshape(lead + (2 * hp, HEAD, HEAD))


def _layer(x, mod, s0, batch, seq, row_len, wts, emit_state):
    d = x.shape[1]
    rows = batch * seq
    rows_per_seg = rows // mod.shape[0]
    d_a = wts["d_a"]
    d_b = wts["d_b"]
    group = d_b // N_GROUPS_B
    lora_w = wts["lora_w"]
    lora_g = wts["lora_g"]
    c_xb = 3 * d_a
    c_gates = c_xb + d_b
    c_dw = c_gates + 2 * d
    c_dg = c_dw + LANES
    assert lora_w == HEAD and wts["lora_a"] == HEAD and lora_g == LANES
    assert seq % SCAN_TILE == 0 and SCAN_TILE % row_len == 0 and row_len & (row_len - 1) == 0

    tm = _pick(rows_per_seg, (512, 256))
    n_in = wts["w_in"].shape[1]
    proj = _inproj(x, mod, wts["norm_mix_g"], wts["w_in"], rows_per_seg, tm, _pick(n_in, (1280, 768, 512, 256)))

    ys, bonus, states = [], None, []
    for dr in range(2):
        outs = _scan(proj, c_dw // LANES, wts["rkv_conv_w"], wts["rkv_conv_b"],
                     wts["decay_up"][dr], wts["decay_base"][dr], wts["iclr_up"][dr], wts["iclr_base"][dr],
                     wts["k_k"], wts["k_a"], wts["r_k"], wts["ones_bd"], wts["tri"][dr],
                     None if s0 is None else s0[:, dr],
                     batch=batch, seq=seq, d_a=d_a, rev=(dr == 1), row_len=row_len,
                     emit_bonus=(dr == 0), emit_state=emit_state)
        ys.append(outs[0])
        if dr == 0:
            bonus = outs[1]
        if emit_state:
            states.append(outs[-1])

    cw = _pick(d_a, (256,))
    ya = _post(ys[0], ys[1], bonus, proj, c_dg // lora_g, wts["gate_up"], wts["lnx_g"], wts["lnx_b"],
               wts["avg_bd"], _pick(rows, (512, 256)), cw)

    z = _chan_dft(proj, wts["chan_cs"], c_xb // group, group, _pick(rows, (512, 256)))
    ct, st = wts["time_cs"][seq]
    yb = _time_dft(z, ct, st, batch, seq, group, _pick(seq, (512, 256)))

    tn = _pick(d, (512, 256))
    assert c_gates % tn == 0 and d % tn == 0
    mixed = _merge(ya, yb, proj, c_gates // tn, (c_gates + d) // tn, wts["w_out_a"], wts["w_fourier"], tm, tn)
    x1 = _outproj(mixed, wts["w_out"], x, mod, rows_per_seg, tm, tn)

    act = _ffn_in(x1, mod, wts["norm_ffn_g"], wts["ffn_w_in"], wts["ffn_conv_w"], wts["ffn_conv_b"],
                  rows_per_seg, row_len, tm, 512)
    y = _ffn_down(act, wts["ffn_w_down"], x1, mod, wts["final_norm_g"], rows_per_seg, tm, 512)
    return y, states


def _tri_blockdiag(n, rev):
    i = np.arange(n)[:, None]
    j = np.arange(n)[None, :]
    same = (i // CHUNK) == (j // CHUNK)
    m = same & ((j >= i) if rev else (j <= i))
    return jnp.asarray(m.astype(np.float32), dtype=BF16)


def _blockdiag_ones(n):
    i = np.arange(n)[:, None] // HEAD
    j = np.arange(n)[None, :] // HEAD
    return jnp.asarray((i == j).astype(np.float32), dtype=BF16)


def kernel(x_prompt, x_sample, state_rwkv, c, c_ctx, ada_w, ada_b, norm_mix_g, w_in, rkv_conv_w, rkv_conv_b,
           decay_up, decay_base, iclr_up, iclr_base, gate_up, k_k, k_a, r_k, lnx_g, lnx_b, w_out_a, w_fourier,
           w_out, norm_ffn_g, ffn_w_in, ffn_conv_w, ffn_conv_b, ffn_w_down, final_norm_g):
    batch, ctx_len, d = x_prompt.shape
    dec_batch, dec_seq, _ = x_sample.shape
    depth = ada_w.shape[0]
    assert depth == 1
    d_a = w_out_a.shape[1]
    d_b = w_fourier.shape[1]
    n_heads = d_a // HEAD
    d_ff = ffn_w_down.shape[1]
    ffp = -(-d_ff // 512) * 512
    lora_w = decay_up.shape[2]
    lora_a = iclr_up.shape[2]
    lora_g = gate_up.shape[1]
    group = d_b // N_GROUPS_B

    cvec = jnp.concatenate([c_ctx[None, :], c], axis=0).astype(F32)
    n_vec = cvec.shape[0]
    cvec = jnp.pad(cvec, ((0, 8 - n_vec), (0, 0)))
    mod = _modulation(cvec, ada_w[0].astype(F32), ada_b[0].astype(F32))[:n_vec].reshape(n_vec, 6, d)

    l = 0
    zpad_w = jnp.zeros((2, LANES - lora_w, d_a), F32)
    zpad_a = jnp.zeros((2, LANES - lora_a, d_a), F32)
    ffn_in = ffn_w_in[l]
    pad_ff = ((0, 0), (0, ffp - d_ff))
    cc, sc = _dft_mats(group)
    wts = {
        "d_a": d_a, "d_b": d_b, "lora_w": lora_w, "lora_a": lora_a, "lora_g": lora_g,
        "norm_mix_g": norm_mix_g[l], "w_in": w_in[l].astype(BF16),
        "rkv_conv_w": rkv_conv_w[l], "rkv_conv_b": rkv_conv_b[l].reshape(1, -1),
        "decay_up": jnp.concatenate([decay_up[l], zpad_w], axis=1),
        "iclr_up": jnp.concatenate([zpad_a, iclr_up[l]], axis=1),
        "decay_base": decay_base[l].reshape(2, 1, d_a), "iclr_base": iclr_base[l].reshape(2, 1, d_a),
        "k_k": k_k[l].reshape(1, d_a), "k_a": k_a[l].reshape(1, d_a), "r_k": r_k[l].reshape(1, d_a),
        "gate_up": gate_up[l].astype(BF16), "lnx_g": lnx_g[l], "lnx_b": lnx_b[l],
        "w_out_a": w_out_a[l].astype(BF16), "w_fourier": w_fourier[l].astype(BF16), "w_out": w_out[l].astype(BF16),
        "norm_ffn_g": norm_ffn_g[l],
        "ffn_w_in": jnp.concatenate([jnp.pad(ffn_in[:, :d_ff], pad_ff), jnp.pad(ffn_in[:, d_ff:], pad_ff)],
                                    axis=1).astype(BF16),
        "ffn_conv_w": jnp.concatenate([jnp.pad(ffn_conv_w[l][:, :d_ff], pad_ff),
                                       jnp.pad(ffn_conv_w[l][:, d_ff:], pad_ff)], axis=1),
        "ffn_conv_b": jnp.concatenate([jnp.pad(ffn_conv_b[l][None, :d_ff], pad_ff),
                                       jnp.pad(ffn_conv_b[l][None, d_ff:], pad_ff)], axis=1),
        "ffn_w_down": jnp.pad(ffn_w_down[l], ((0, ffp - d_ff), (0, 0))).astype(BF16),
        "final_norm_g": final_norm_g,
        "ones_bd": _blockdiag_ones(LANES), "avg_bd": _blockdiag_ones(256),
        "tri": (_tri_blockdiag(SCAN_TILE, False), _tri_blockdiag(SCAN_TILE, True)),
        "chan_cs": jnp.concatenate([cc, sc], axis=1).astype(BF16),
        "time_cs": {n: tuple(m.astype(BF16) for m in _dft_mats(n)) for n in {ctx_len, dec_seq}},
    }

    y_ctx, st_ctx = _layer(x_prompt.reshape(batch * ctx_len, d), mod[:1], None, batch, ctx_len, ctx_len,
                           wts, emit_state=True)
    s0 = _pair_layout(jnp.swapaxes(state_rwkv[:, l].astype(F32), -1, -2))
    y_lat, _ = _layer(x_sample.reshape(dec_batch * dec_seq, d), mod[1:], s0, dec_batch, dec_seq, GRID_W,
                      wts, emit_state=False)

    new_state = jnp.stack([_unpair_layout(st_ctx[0]), _unpair_layout(st_ctx[1])], axis=1)
    new_state = jnp.swapaxes(new_state, -1, -2)[:, None].astype(x_prompt.dtype)
    return (y_ctx.reshape(batch, ctx_len, d), y_lat.reshape(dec_batch, dec_seq, d), new_state)
```

```python
import functools
import math

import numpy as np
import jax
import jax.numpy as jnp
from jax import lax
from jax.experimental import pallas as pl
from jax.experimental.pallas import tpu as pltpu

F32 = jnp.float32
BF16 = jnp.bfloat16

HEAD = 64
LANES = 128
MXU_COLS = 256
FFN_SUB_ROWS = 256
CHUNK = 64
SCAN_TILE = 256
SCAN_PAIRS = 8
GRID_W = 64
N_GROUPS_B = 4
RMS_EPS = 1e-6
GN_EPS = 64e-5
VMEM_LIMIT = 56 * 1024 * 1024

NN = ((1,), (0,))
NT = ((1,), (1,))
TN = ((0,), (0,))


def _cparams(sem):
    return pltpu.CompilerParams(dimension_semantics=sem, vmem_limit_bytes=VMEM_LIMIT)


def _dot(a, b, dims=NN):
    return lax.dot_general(a, b, (dims, ((), ())), preferred_element_type=F32)


def _split2(x):
    hi = x.astype(BF16)
    lo = (x - hi.astype(F32)).astype(BF16)
    return hi, lo


def _split3(x):
    hi = x.astype(BF16)
    r1 = x - hi.astype(F32)
    mid = r1.astype(BF16)
    lo = (r1 - mid.astype(F32)).astype(BF16)
    return hi, mid, lo


def _dot3(a, b, dims=NN):
    ah, al = _split2(a)
    bh, bl = _split2(b)
    return _dot(ah, bh, dims) + (_dot(ah, bl, dims) + _dot(al, bh, dims))


def _dot_exact_rhs(a, b_bf16, dims=NN):
    ah, al = _split2(a)
    return _dot(ah, b_bf16, dims) + _dot(al, b_bf16, dims)


def _sigmoid(x):
    return 1.0 / (1.0 + jnp.exp(-x))


def _mod_kernel(c_ref, w_ref, b_ref, o_ref):
    c = c_ref[...]
    s = c * _sigmoid(c)
    o_ref[...] = jnp.dot(s, w_ref[...], preferred_element_type=F32,
                         precision=lax.Precision.HIGHEST) + b_ref[...]


def _modulation(cvec, ada_w, ada_b):
    d, n = ada_w.shape
    tn = 1024 if n % 1024 == 0 else 512
    return pl.pallas_call(
        _mod_kernel,
        grid=(n // tn,),
        in_specs=[pl.BlockSpec((8, d), lambda j: (0, 0)),
                  pl.BlockSpec((d, tn), lambda j: (0, j)),
                  pl.BlockSpec((1, tn), lambda j: (0, j))],
        out_specs=pl.BlockSpec((8, tn), lambda j: (0, j)),
        out_shape=jax.ShapeDtypeStruct((8, n), F32),
        compiler_params=_cparams(("arbitrary",)),
        name="modulation",
    )(cvec, ada_w, ada_b.reshape(1, n))


def _rms_mod(x, g, shift, scale):
    ms = jnp.mean(x * x, axis=-1, keepdims=True)
    y = x * lax.rsqrt(ms + RMS_EPS) * g
    return y * (1.0 + scale) + shift


def _inproj_kernel(x_ref, mod_ref, g_ref, w_ref, o_ref, h_ref):
    @pl.when(pl.program_id(1) == 0)
    def _():
        h = _rms_mod(x_ref[...], g_ref[...], mod_ref[0:1, :], mod_ref[1:2, :])
        h_ref[...] = h.astype(BF16)

    o_ref[...] = jnp.dot(h_ref[...], w_ref[...], preferred_element_type=F32)


def _inproj(x, mod, g, w, rows_per_seg, tm, tn):
    r, d = x.shape
    n = w.shape[1]
    tps = rows_per_seg // tm
    return pl.pallas_call(
        _inproj_kernel,
        grid=(r // tm, n // tn),
        in_specs=[pl.BlockSpec((tm, d), lambda i, j: (i, 0)),
                  pl.BlockSpec((None, 6, d), lambda i, j: (i // tps, 0, 0)),
                  pl.BlockSpec((1, d), lambda i, j: (0, 0)),
                  pl.BlockSpec((d, tn), lambda i, j: (0, j))],
        out_specs=pl.BlockSpec((tm, tn), lambda i, j: (i, j)),
        out_shape=jax.ShapeDtypeStruct((r, n), F32),
        scratch_shapes=[pltpu.VMEM((tm, d), BF16)],
        compiler_params=_cparams(("parallel", "arbitrary")),
        name="inproj",
    )(x, mod, g.reshape(1, d), w)


def _conv3(x, w_ref, b_ref, pos, row_len):
    n = x.shape[0]
    prev = jnp.where(pos == 0, 0.0, pltpu.roll(x, 1, 0))
    nxt = jnp.where(pos == row_len - 1, 0.0, pltpu.roll(x, n - 1, 0))
    return prev * w_ref[0:1, :] + x * w_ref[1:2, :] + nxt * w_ref[2:3, :] + b_ref[...]


def _pair_diag(x, lane_lo):
    return jnp.concatenate([jnp.where(lane_lo, x, 0.0), jnp.where(lane_lo, 0.0, x)], axis=0)


def _scan_kernel(*refs, rev, row_len, has_init, emit_bonus, emit_state, n_pairs):
    (rp_ref, kp_ref, vp_ref, dwda_ref, cwr_ref, cwk_ref, cwv_ref, cbr_ref, cbk_ref, cbv_ref,
     decup_ref, decb_ref, iclup_ref, iclb_ref, kkw_ref, kaw_ref, rkw_ref, ones_ref, tri_ref) = refs[:19]
    rest = list(refs[19:])
    s0_ref = rest.pop(0) if has_init else None
    y_ref = rest.pop(0)
    bonus_ref = rest.pop(0) if emit_bonus else None
    sfin_ref = rest.pop(0) if emit_state else None
    s_ref = rest.pop(0)

    t = pl.program_id(2)
    n_t = pl.num_programs(2)
    tt, width = rp_ref.shape

    @pl.when(t == 0)
    def _():
        if has_init:
            s_ref[...] = s0_ref[...]
        else:
            s_ref[...] = jnp.zeros_like(s_ref)

    ones_bd = ones_ref[...]

    def head_sum(x):
        return jnp.concatenate([_dot_exact_rhs(x[:, p * LANES:(p + 1) * LANES], ones_bd) for p in range(n_pairs)],
                               axis=1)

    pos = lax.broadcasted_iota(jnp.int32, (tt, width), 0) & (row_len - 1)
    r = _conv3(rp_ref[...], cwr_ref, cbr_ref, pos, row_len)
    k = _conv3(kp_ref[...], cwk_ref, cbk_ref, pos, row_len)
    v = _conv3(vp_ref[...], cwv_ref, cbv_ref, pos, row_len)

    kk = k * kkw_ref[...]
    kk = kk * lax.rsqrt(head_sum(kk * kk) + 1e-12)
    dwda = dwda_ref[...]
    w_logit = decb_ref[...] + _dot3(jnp.tanh(dwda), decup_ref[...])
    logw = (-math.exp(-0.5)) * _sigmoid(w_logit)
    a = _sigmoid(iclb_ref[...] + _dot3(dwda, iclup_ref[...]))
    kd = k * (1.0 + (a - 1.0) * kaw_ref[...])
    bb = kk * a
    if emit_bonus:
        bonus_ref[...] = head_sum(r * k * rkw_ref[...]) * v

    tri = tri_ref[...]
    l1, l2, l3 = _split3(logw)
    cum = _dot(tri, l1) + (_dot(tri, l2) + _dot(tri, l3))

    row = lax.broadcasted_iota(jnp.int32, (CHUNK, LANES), 0)
    lane = lax.broadcasted_iota(jnp.int32, (CHUNK, LANES), 1)
    col = lane & (HEAD - 1)
    lane_lo = lane < HEAD
    if rev:
        strict = col > row
        incl = col >= row
    else:
        strict = col < row
        incl = col <= row
    same16 = (row >> 4) == (col >> 4)
    same32 = (row >> 5) == (col >> 5)
    off16 = jnp.logical_and(same32, jnp.logical_not(same16))
    eye = jnp.where(col == row, 1.0, 0.0).astype(F32)
    eye_t = (lax.broadcasted_iota(jnp.int32, (LANES, LANES), 0)
             == lax.broadcasted_iota(jnp.int32, (LANES, LANES), 1)).astype(BF16)

    def b16(x):
        return x.astype(BF16)

    def pd(x):
        return b16(_pair_diag(x, lane_lo))

    def mm(p, q_pd):
        return _dot(b16(p), q_pd)

    n_chunks = tt // CHUNK
    order = list(range(n_chunks - 1, -1, -1) if rev else range(n_chunks))
    last_row = 0 if rev else CHUNK - 1
    streams = [(c, p) for c in order for p in range(n_pairs)]

    q = {}
    for c in order:
        sl = slice(c * CHUNK, (c + 1) * CHUNK)
        cin = cum[sl]
        c_end = cin[last_row:last_row + 1, :]
        g_inv = jnp.exp(-cin)
        g_rat = jnp.exp(c_end - cin)
        rt = r[sl] * jnp.exp(cin)
        kkt = kk[sl] * jnp.exp(cin - logw[sl])
        kt = kd[sl] * g_inv
        bt = bb[sl] * g_inv
        kh = kd[sl] * g_rat
        bh = bb[sl] * g_rat
        vc = v[sl]
        for p in range(n_pairs):
            ls = slice(p * LANES, (p + 1) * LANES)
            q[c, p] = dict(sl=sl, ls=ls, c_end=c_end[:, ls], rt=b16(rt[:, ls]), kkt=kkt[:, ls], kt=kt[:, ls],
                           bt=bt[:, ls], bkh=b16(jnp.concatenate([bh[:, ls], kh[:, ls]], axis=0)), vc=vc[:, ls])
    for st in streams:
        d = q[st]
        x = jnp.concatenate([b16(d["kkt"]), d["rt"]], axis=0)
        sb = _dot(x, pd(d["bt"]), NT)
        sk = _dot(x, pd(d["kt"]), NT)
        d["a_b"] = jnp.where(strict, sb[:CHUNK], 0.0)
        d["a_k"] = jnp.where(strict, sk[:CHUNK], 0.0)
        d["m_bk"] = b16(jnp.concatenate([jnp.where(incl, sb[CHUNK:], 0.0), jnp.where(incl, sk[CHUNK:], 0.0)], axis=1))
        d["n0"] = jnp.where(same16, d["a_b"], 0.0)
        d["vpd"] = pd(d["vc"])
    for st in streams:
        d = q[st]
        d["n2"] = mm(d["n0"], pd(d["n0"]))
        d["akv"] = mm(d["a_k"], d["vpd"])
    for st in streams:
        d = q[st]
        t1 = eye - d["n0"]
        both = _dot(b16(jnp.concatenate([d["n2"], t1], axis=0)), pd(d["n2"]))
        d["n4"] = both[:CHUNK]
        d["t"] = t1 + both[CHUNK:]
    for st in streams:
        d = q[st]
        both = _dot(b16(jnp.concatenate([d["n4"], d["t"]], axis=0)), pd(d["n4"]))
        d["n8"] = both[:CHUNK]
        d["t"] = d["t"] + both[CHUNK:]
    for st in streams:
        d = q[st]
        d["t"] = d["t"] + mm(d["t"], pd(d["n8"]))
    for st in streams:
        d = q[st]
        d["p"] = mm(d["t"], pd(jnp.where(off16, d["a_b"], 0.0)))
    for st in streams:
        d = q[st]
        d["t"] = d["t"] - mm(d["p"], pd(d["t"]))
    for st in streams:
        d = q[st]
        d["p"] = mm(d["t"], pd(jnp.where(same32, 0.0, d["a_b"])))
    for st in streams:
        d = q[st]
        d["t"] = d["t"] - mm(d["p"], pd(d["t"]))
    for st in streams:
        d = q[st]
        w12 = _dot(b16(d["t"]), jnp.concatenate([pd(d["kkt"]), pd(d["akv"])], axis=1))
        d["w1r"] = jnp.concatenate([b16(w12[:, :LANES]), d["rt"]], axis=0)
        d["w2"] = w12[:, LANES:]
        d["bk_t"] = b16(_dot(eye_t, d["bkh"], NT))
        c1, c2, c3 = _split3(jnp.where(col == row, d["c_end"], 0.0))
        d["g_col"] = jnp.exp(_dot(c1, ones_bd) + (_dot(c2, ones_bd) + _dot(c3, ones_bd)))

    s = [s_ref[p] for p in range(n_pairs)]
    for c in order:
        for p in range(n_pairs):
            d = q[c, p]
            xs = _dot(d["w1r"], pd(s[p]))
            d["u"] = -(xs[:CHUNK] + d["w2"])
            d["ys"] = xs[CHUNK:]
            full = _dot(d["bk_t"], b16(jnp.concatenate([d["u"], d["vc"]], axis=0)))
            s[p] = s[p] * d["g_col"] + jnp.where(lane_lo, full[:CHUNK], full[CHUNK:])
    for st in streams:
        d = q[st]
        y_ref[d["sl"], d["ls"]] = d["ys"] + _dot(d["m_bk"], jnp.concatenate([pd(d["u"]), d["vpd"]], axis=0))

    for p in range(n_pairs):
        s_ref[p] = s[p]
    if emit_state:
        @pl.when(t == n_t - 1)
        def _():
            for p in range(n_pairs):
                h1, h2, h3 = _split3(s[p])
                tr = _dot(eye_t, h1, NT) + (_dot(eye_t, h2, NT) + _dot(eye_t, h3, NT))
                sfin_ref[2 * p] = tr[:HEAD]
                sfin_ref[2 * p + 1] = tr[HEAD:]


def _scan(proj, dwda_blk, conv_w, conv_b, dec_up, dec_base, icl_up, icl_base, k_k, k_a, r_k,
          ones_bd, tri, s0, *, batch, seq, d_a, rev, row_len, emit_bonus, emit_state):
    n_t = seq // SCAN_TILE
    n_p = d_a // LANES
    pps = SCAN_PAIRS
    assert n_p % pps == 0
    n_g = n_p // pps
    width = pps * LANES
    tt = SCAN_TILE
    rows = batch * seq

    def tile(b, t):
        return b * n_t + ((n_t - 1 - t) if rev else t)

    def colspec(base):
        return pl.BlockSpec((tt, width), lambda b, g, t, base=base: (tile(b, t), base + g))

    def wspec(nrow, base=0):
        return pl.BlockSpec((nrow, width), lambda b, g, t, base=base: (0, base + g))

    const2 = lambda shape: pl.BlockSpec(shape, lambda b, g, t: (0, 0))
    state_spec = pl.BlockSpec((None, pps, HEAD, LANES), lambda b, g, t: (b, g, 0, 0))
    in_specs = [colspec(0), colspec(n_g), colspec(2 * n_g),
                pl.BlockSpec((tt, LANES), lambda b, g, t: (tile(b, t), dwda_blk)),
                wspec(3, 0), wspec(3, n_g), wspec(3, 2 * n_g),
                wspec(1, 0), wspec(1, n_g), wspec(1, 2 * n_g),
                wspec(LANES), wspec(1), wspec(LANES), wspec(1), wspec(1), wspec(1), wspec(1),
                const2((LANES, LANES)), const2((tt, tt))]
    args = [proj, proj, proj, proj, conv_w, conv_w, conv_w, conv_b, conv_b, conv_b,
            dec_up, dec_base, icl_up, icl_base, k_k, k_a, r_k, ones_bd, tri]
    has_init = s0 is not None
    if has_init:
        in_specs.append(state_spec)
        args.append(s0)
    out_specs = [pl.BlockSpec((tt, width), lambda b, g, t: (tile(b, t), g))]
    out_shape = [jax.ShapeDtypeStruct((rows, d_a), F32)]
    if emit_bonus:
        out_specs.append(pl.BlockSpec((tt, width), lambda b, g, t: (tile(b, t), g)))
        out_shape.append(jax.ShapeDtypeStruct((rows, d_a), F32))
    if emit_state:
        out_specs.append(pl.BlockSpec((None, 2 * pps, HEAD, HEAD), lambda b, g, t: (b, g, 0, 0)))
        out_shape.append(jax.ShapeDtypeStruct((batch, 2 * n_p, HEAD, HEAD), F32))
    kern = functools.partial(_scan_kernel, rev=rev, row_len=row_len, has_init=has_init,
                             emit_bonus=emit_bonus, emit_state=emit_state, n_pairs=pps)
    return pl.pallas_call(
        kern,
        grid=(batch, n_g, n_t),
        in_specs=in_specs,
        out_specs=out_specs,
        out_shape=out_shape,
        scratch_shapes=[pltpu.VMEM((pps, HEAD, LANES), F32)],
        compiler_params=_cparams(("parallel", "parallel", "arbitrary")),
        name="rwkv7_scan_rev" if rev else "rwkv7_scan_fwd",
    )(*args)


def _chan_dft_kernel(x_ref, w_ref, o_ref):
    o_ref[...] = jnp.dot(x_ref[...].astype(BF16), w_ref[...], preferred_element_type=F32).astype(BF16)


def _chan_dft(proj, w_cs, xb_blk, group, tm):
    r = proj.shape[0]
    return pl.pallas_call(
        _chan_dft_kernel,
        grid=(r // tm, N_GROUPS_B),
        in_specs=[pl.BlockSpec((tm, group), lambda i, g: (i, xb_blk + g)),
                  pl.BlockSpec((group, 2 * group), lambda i, g: (0, 0))],
        out_specs=pl.BlockSpec((tm, 2 * group), lambda i, g: (i, g)),
        out_shape=jax.ShapeDtypeStruct((r, N_GROUPS_B * 2 * group), BF16),
        compiler_params=_cparams(("parallel", "arbitrary")),
        name="fourier_channels",
    )(proj, w_cs)


def _time_dft_kernel(ct_ref, st_ref, z_ref, o_ref, *, group):
    zc = z_ref[:, :group]
    zs = z_ref[:, group:]
    y = jnp.dot(ct_ref[...], zc, preferred_element_type=F32) - jnp.dot(st_ref[...], zs, preferred_element_type=F32)
    o_ref[...] = y.astype(BF16)


def _time_dft(z, ct, st, batch, seq, group, tm):
    n_m = seq // tm
    return pl.pallas_call(
        functools.partial(_time_dft_kernel, group=group),
        grid=(batch, N_GROUPS_B, n_m),
        in_specs=[pl.BlockSpec((tm, seq), lambda b, g, m: (m, 0)),
                  pl.BlockSpec((tm, seq), lambda b, g, m: (m, 0)),
                  pl.BlockSpec((seq, 2 * group), lambda b, g, m: (b, g))],
        out_specs=pl.BlockSpec((tm, group), lambda b, g, m: (b * n_m + m, g)),
        out_shape=jax.ShapeDtypeStruct((batch * seq, N_GROUPS_B * group), BF16),
        compiler_params=_cparams(("parallel", "parallel", "arbitrary")),
        name="fourier_positions",
    )(ct, st, z)


def _dft_mats(n):
    scale = 1.0 / math.sqrt(n)
    m = math.isqrt(n)
    if m * m != n or n <= 1024:
        i = lax.broadcasted_iota(jnp.int32, (n, n), 0)
        j = lax.broadcasted_iota(jnp.int32, (n, n), 1)
        ang = ((i * j) % n).astype(F32) * (2.0 * math.pi / n)
        return jnp.cos(ang) * scale, jnp.sin(ang) * scale
    t = lax.broadcasted_iota(jnp.int32, (m, n), 0)
    f = lax.broadcasted_iota(jnp.int32, (m, n), 1)
    a1 = ((m * t * f) % n).astype(F32) * (2.0 * math.pi / n)
    a2 = ((t * f) % n).astype(F32) * (2.0 * math.pi / n)
    c1, s1 = (jnp.cos(a1) * scale)[:, None, :], (jnp.sin(a1) * scale)[:, None, :]
    c2, s2 = jnp.cos(a2)[None, :, :], jnp.sin(a2)[None, :, :]
    return (c1 * c2 - s1 * s2).reshape(n, n), (s1 * c2 + c1 * s2).reshape(n, n)


def _post_kernel(yf_ref, yb_ref, bonus_ref, dg_ref, gup_ref, lg_ref, lb_ref, avg_ref, o_ref):
    y = yf_ref[...] + yb_ref[...]
    avg = avg_ref[...]
    mu = _dot_exact_rhs(y, avg) * (1.0 / HEAD)
    d = y - mu
    var = _dot_exact_rhs(d * d, avg) * (1.0 / HEAD)
    yn = d * lax.rsqrt(var + GN_EPS) * lg_ref[...] + lb_ref[...]
    g = jnp.dot(_sigmoid(dg_ref[...]).astype(BF16), gup_ref[...], preferred_element_type=F32)
    o_ref[...] = ((yn + bonus_ref[...]) * g).astype(BF16)


def _post(yf, yb, bonus, proj, dg_blk, gate_up, lnx_g, lnx_b, avg, tm, cw):
    r, d_a = yf.shape
    lg = gate_up.shape[0]
    tile = pl.BlockSpec((tm, cw), lambda i, j: (i, j))
    vec = pl.BlockSpec((1, cw), lambda i, j: (0, j))
    return pl.pallas_call(
        _post_kernel,
        grid=(r // tm, d_a // cw),
        in_specs=[tile, tile, tile,
                  pl.BlockSpec((tm, lg), lambda i, j: (i, dg_blk)),
                  pl.BlockSpec((lg, cw), lambda i, j: (0, j)),
                  vec, vec,
                  pl.BlockSpec((cw, cw), lambda i, j: (0, 0))],
        out_specs=tile,
        out_shape=jax.ShapeDtypeStruct((r, d_a), BF16),
        compiler_params=_cparams(("parallel", "arbitrary")),
        name="headnorm_gate",
    )(yf, yb, bonus, proj, gate_up, lnx_g.reshape(1, d_a), lnx_b.reshape(1, d_a), avg)


def _mix_out_kernel(ya_ref, yb_ref, ga0_ref, ga1_ref, gb0_ref, gb1_ref, wa_ref, wf_ref, wo_ref, x_ref, mod_ref,
                    o_ref):
    half = ga0_ref.shape[1]
    ya = ya_ref[...]
    yb = yb_ref[...]
    parts = []
    for q, (ga_ref, gb_ref) in enumerate(((ga0_ref, gb0_ref), (ga1_ref, gb1_ref))):
        cols = slice(q * half, (q + 1) * half)
        pa = jnp.dot(ya, wa_ref[:, cols], preferred_element_type=F32)
        pb = jnp.dot(yb, wf_ref[:, cols], preferred_element_type=F32)
        parts.append((_sigmoid(ga_ref[...]) * pa + _sigmoid(gb_ref[...]) * pb).astype(BF16))
    mixed = jnp.concatenate(parts, axis=1)
    p = jnp.dot(mixed, wo_ref[...], preferred_element_type=F32)
    o_ref[...] = x_ref[...] + mod_ref[2:3, :] * p


def _mix_out(ya, yb, proj, c_gates, w_a, w_f, w_o, x, mod, rows_per_seg, tm):
    r, d = x.shape
    d_a = ya.shape[1]
    d_b = yb.shape[1]
    half = d // 2
    assert c_gates % half == 0
    gblk = c_gates // half
    tps = rows_per_seg // tm
    resident = lambda shape: pl.BlockSpec(shape, lambda i: (0, 0), pipeline_mode=pl.Buffered(1))
    gate = lambda k: pl.BlockSpec((tm, half), lambda i, k=k: (i, gblk + k))
    return pl.pallas_call(
        _mix_out_kernel,
        grid=(r // tm,),
        in_specs=[pl.BlockSpec((tm, d_a), lambda i: (i, 0)),
                  pl.BlockSpec((tm, d_b), lambda i: (i, 0)),
                  gate(0), gate(1), gate(2), gate(3),
                  resident((d_a, d)), resident((d_b, d)), resident((d, d)),
                  pl.BlockSpec((tm, d), lambda i: (i, 0)),
                  pl.BlockSpec((None, 6, d), lambda i: (i // tps, 0, 0))],
        out_specs=pl.BlockSpec((tm, d), lambda i: (i, 0)),
        out_shape=jax.ShapeDtypeStruct((r, d), F32),
        compiler_params=_cparams(("parallel",)),
        name="merge_outproj",
    )(ya, yb, proj, proj, proj, proj, w_a, w_f, w_o, x, mod)


def _ffn_in_kernel(x_ref, mod_ref, g_ref, wg_ref, wv_ref, cwg_ref, cwv_ref, cbg_ref, cbv_ref, o_ref, h_ref,
                   *, row_len):
    @pl.when(pl.program_id(1) == 0)
    def _():
        h = _rms_mod(x_ref[...], g_ref[...], mod_ref[3:4, :], mod_ref[4:5, :])
        h_ref[...] = h.astype(BF16)

    tm, tn = o_ref.shape
    sub_n = MXU_COLS
    sub_m = min(tm, FFN_SUB_ROWS)
    assert sub_m % row_len == 0
    pos = lax.broadcasted_iota(jnp.int32, (sub_m, sub_n), 0) & (row_len - 1)
    blocks = [(slice(m * sub_m, (m + 1) * sub_m), slice(q * sub_n, (q + 1) * sub_n))
              for m in range(tm // sub_m) for q in range(tn // sub_n)]

    def matmuls(blk):
        rows, cols = blk
        h = h_ref[rows, :]
        return (jnp.dot(h, wg_ref[:, cols], preferred_element_type=F32),
                jnp.dot(h, wv_ref[:, cols], preferred_element_type=F32))

    def epilogue(blk, ug, uv):
        rows, cols = blk
        ug = _conv3(ug, cwg_ref.at[:, cols], cbg_ref.at[:, cols], pos, row_len)
        uv = _conv3(uv, cwv_ref.at[:, cols], cbv_ref.at[:, cols], pos, row_len)
        o_ref[rows, cols] = (ug * _sigmoid(ug) * uv).astype(BF16)

    cur = matmuls(blocks[0])
    for i, blk in enumerate(blocks):
        nxt = matmuls(blocks[i + 1]) if i + 1 < len(blocks) else None
        epilogue(blk, *cur)
        cur = nxt


def _ffn_in(x, mod, g, w2, cw2, cb2, rows_per_seg, row_len, tm, tn):
    r, d = x.shape
    ffp = w2.shape[1] // 2
    nj = ffp // tn
    tps = rows_per_seg // tm
    return pl.pallas_call(
        functools.partial(_ffn_in_kernel, row_len=row_len),
        grid=(r // tm, nj),
        in_specs=[pl.BlockSpec((tm, d), lambda i, j: (i, 0)),
                  pl.BlockSpec((None, 6, d), lambda i, j: (i // tps, 0, 0)),
                  pl.BlockSpec((1, d), lambda i, j: (0, 0)),
                  pl.BlockSpec((d, tn), lambda i, j: (0, j)),
                  pl.BlockSpec((d, tn), lambda i, j: (0, nj + j)),
                  pl.BlockSpec((3, tn), lambda i, j: (0, j)),
                  pl.BlockSpec((3, tn), lambda i, j: (0, nj + j)),
                  pl.BlockSpec((1, tn), lambda i, j: (0, j)),
                  pl.BlockSpec((1, tn), lambda i, j: (0, nj + j))],
        out_specs=pl.BlockSpec((tm, tn), lambda i, j: (i, j)),
        out_shape=jax.ShapeDtypeStruct((r, ffp), BF16),
        scratch_shapes=[pltpu.VMEM((tm, d), BF16)],
        compiler_params=_cparams(("parallel", "arbitrary")),
        name="ffn_in_conv_gate",
    )(x, mod, g.reshape(1, d), w2, w2, cw2, cw2, cb2, cb2)


def _ffn_down_kernel(a_ref, w_ref, x_ref, mod_ref, g_ref, o_ref):
    x2 = x_ref[...] + mod_ref[5:6, :] * jnp.dot(a_ref[...], w_ref[...], preferred_element_type=F32)
    ms = jnp.mean(x2 * x2, axis=-1, keepdims=True)
    o_ref[...] = x2 * lax.rsqrt(ms + RMS_EPS) * g_ref[...]


def _ffn_down(act, w, x, mod, g, rows_per_seg, tm):
    r, d = x.shape
    ffp = act.shape[1]
    tps = rows_per_seg // tm
    return pl.pallas_call(
        _ffn_down_kernel,
        grid=(r // tm,),
        in_specs=[pl.BlockSpec((tm, ffp), lambda i: (i, 0)),
                  pl.BlockSpec((ffp, d), lambda i: (0, 0), pipeline_mode=pl.Buffered(1)),
                  pl.BlockSpec((tm, d), lambda i: (i, 0)),
                  pl.BlockSpec((None, 6, d), lambda i: (i // tps, 0, 0)),
                  pl.BlockSpec((1, d), lambda i: (0, 0))],
        out_specs=pl.BlockSpec((tm, d), lambda i: (i, 0)),
        out_shape=jax.ShapeDtypeStruct((r, d), F32),
        compiler_params=_cparams(("parallel",)),
        name="ffn_down_final_norm",
    )(act, w, x, mod, g.reshape(1, d))


def _pick(n, prefs):
    for p in prefs:
        if n % p == 0:
            return p
    raise ValueError(f"no tile in {prefs} divides {n}")


def _pair_layout(s):
    lead = s.shape[:-3]
    h = s.shape[-3]
    s = s.reshape(lead + (h // 2, 2, HEAD, HEAD))
    s = jnp.swapaxes(s, -3, -2)
    return s.reshape(lead + (h // 2, HEAD, 2 * HEAD))


def _layer(x, mod, s0, batch, seq, row_len, wts, emit_state):
    d = x.shape[1]
    rows = batch * seq
    rows_per_seg = rows // mod.shape[0]
    d_a = wts["d_a"]
    d_b = wts["d_b"]
    group = d_b // N_GROUPS_B
    lora_w = wts["lora_w"]
    lora_g = wts["lora_g"]
    c_xb = 3 * d_a
    c_gates = c_xb + d_b
    c_dw = c_gates + 2 * d
    c_dg = c_dw + LANES
    assert lora_w == HEAD and wts["lora_a"] == HEAD and lora_g == LANES
    assert seq % SCAN_TILE == 0 and SCAN_TILE % row_len == 0 and row_len & (row_len - 1) == 0

    tm = _pick(rows_per_seg, (1024, 512, 256))
    n_in = wts["w_in"].shape[1]
    proj = _inproj(x, mod, wts["norm_mix_g"], wts["w_in"], rows_per_seg, tm, _pick(n_in, (1280, 768, 512, 256)))

    ys, bonus, states = [], None, []
    for dr in range(2):
        outs = _scan(proj, c_dw // LANES, wts["rkv_conv_w"], wts["rkv_conv_b"],
                     wts["decay_up"][dr], wts["decay_base"][dr], wts["iclr_up"][dr], wts["iclr_base"][dr],
                     wts["k_k"], wts["k_a"], wts["r_k"], wts["ones_bd"], wts["tri"][dr],
                     None if s0 is None else s0[:, dr],
                     batch=batch, seq=seq, d_a=d_a, rev=(dr == 1), row_len=row_len,
                     emit_bonus=(dr == 0), emit_state=emit_state)
        ys.append(outs[0])
        if dr == 0:
            bonus = outs[1]
        if emit_state:
            states.append(outs[-1])

    cw = _pick(d_a, (256,))
    ya = _post(ys[0], ys[1], bonus, proj, c_dg // lora_g, wts["gate_up"], wts["lnx_g"], wts["lnx_b"],
               wts["avg_bd"], _pick(rows, (512, 256)), cw)

    z = _chan_dft(proj, wts["chan_cs"], c_xb // group, group, _pick(rows, (512, 256)))
    ct, st = wts["time_cs"][seq]
    yb = _time_dft(z, ct, st, batch, seq, group, _pick(seq, (512, 256)))

    x1 = _mix_out(ya, yb, proj, c_gates, wts["w_out_a"], wts["w_fourier"], wts["w_out"], x, mod,
                  rows_per_seg, 256)

    act = _ffn_in(x1, mod, wts["norm_ffn_g"], wts["ffn_w_in"], wts["ffn_conv_w"], wts["ffn_conv_b"],
                  rows_per_seg, row_len, tm, 512)
    y = _ffn_down(act, wts["ffn_w_down"], x1, mod, wts["final_norm_g"], rows_per_seg, 256)
    return y, states


def _tri_blockdiag(n, rev):
    i = np.arange(n)[:, None]
    j = np.arange(n)[None, :]
    same = (i // CHUNK) == (j // CHUNK)
    m = same & ((j >= i) if rev else (j <= i))
    return jnp.asarray(m.astype(np.float32), dtype=BF16)


def _blockdiag_ones(n):
    i = np.arange(n)[:, None] // HEAD
    j = np.arange(n)[None, :] // HEAD
    return jnp.asarray((i == j).astype(np.float32), dtype=BF16)


def kernel(x_prompt, x_sample, state_rwkv, c, c_ctx, ada_w, ada_b, norm_mix_g, w_in, rkv_conv_w, rkv_conv_b,
           decay_up, decay_base, iclr_up, iclr_base, gate_up, k_k, k_a, r_k, lnx_g, lnx_b, w_out_a, w_fourier,
           w_out, norm_ffn_g, ffn_w_in, ffn_conv_w, ffn_conv_b, ffn_w_down, final_norm_g):
    batch, ctx_len, d = x_prompt.shape
    dec_batch, dec_seq, _ = x_sample.shape
    depth = ada_w.shape[0]
    assert depth == 1
    d_a = w_out_a.shape[1]
    d_b = w_fourier.shape[1]
    n_heads = d_a // HEAD
    d_ff = ffn_w_down.shape[1]
    ffp = -(-d_ff // 512) * 512
    lora_w = decay_up.shape[2]
    lora_a = iclr_up.shape[2]
    lora_g = gate_up.shape[1]
    group = d_b // N_GROUPS_B

    cvec = jnp.concatenate([c_ctx[None, :], c], axis=0).astype(F32)
    n_vec = cvec.shape[0]
    cvec = jnp.pad(cvec, ((0, 8 - n_vec), (0, 0)))
    mod = _modulation(cvec, ada_w[0].astype(F32), ada_b[0].astype(F32))[:n_vec].reshape(n_vec, 6, d)

    l = 0
    zpad_w = jnp.zeros((2, LANES - lora_w, d_a), F32)
    zpad_a = jnp.zeros((2, LANES - lora_a, d_a), F32)
    ffn_in = ffn_w_in[l]
    pad_ff = ((0, 0), (0, ffp - d_ff))
    cc, sc = _dft_mats(group)
    wts = {
        "d_a": d_a, "d_b": d_b, "lora_w": lora_w, "lora_a": lora_a, "lora_g": lora_g,
        "norm_mix_g": norm_mix_g[l], "w_in": w_in[l].astype(BF16),
        "rkv_conv_w": rkv_conv_w[l], "rkv_conv_b": rkv_conv_b[l].reshape(1, -1),
        "decay_up": jnp.concatenate([decay_up[l], zpad_w], axis=1),
        "iclr_up": jnp.concatenate([zpad_a, iclr_up[l]], axis=1),
        "decay_base": decay_base[l].reshape(2, 1, d_a), "iclr_base": iclr_base[l].reshape(2, 1, d_a),
        "k_k": k_k[l].reshape(1, d_a), "k_a": k_a[l].reshape(1, d_a), "r_k": r_k[l].reshape(1, d_a),
        "gate_up": gate_up[l].astype(BF16), "lnx_g": lnx_g[l], "lnx_b": lnx_b[l],
        "w_out_a": w_out_a[l].astype(BF16), "w_fourier": w_fourier[l].astype(BF16), "w_out": w_out[l].astype(BF16),
        "norm_ffn_g": norm_ffn_g[l],
        "ffn_w_in": jnp.concatenate([jnp.pad(ffn_in[:, :d_ff], pad_ff), jnp.pad(ffn_in[:, d_ff:], pad_ff)],
                                    axis=1).astype(BF16),
        "ffn_conv_w": jnp.concatenate([jnp.pad(ffn_conv_w[l][:, :d_ff], pad_ff),
                                       jnp.pad(ffn_conv_w[l][:, d_ff:], pad_ff)], axis=1),
        "ffn_conv_b": jnp.concatenate([jnp.pad(ffn_conv_b[l][None, :d_ff], pad_ff),
                                       jnp.pad(ffn_conv_b[l][None, d_ff:], pad_ff)], axis=1),
        "ffn_w_down": jnp.pad(ffn_w_down[l], ((0, ffp - d_ff), (0, 0))).astype(BF16),
        "final_norm_g": final_norm_g,
        "ones_bd": _blockdiag_ones(LANES), "avg_bd": _blockdiag_ones(256),
        "tri": (_tri_blockdiag(SCAN_TILE, False), _tri_blockdiag(SCAN_TILE, True)),
        "chan_cs": jnp.concatenate([cc, sc], axis=1).astype(BF16),
        "time_cs": {n: tuple(m.astype(BF16) for m in _dft_mats(n)) for n in {ctx_len, dec_seq}},
    }

    y_ctx, st_ctx = _layer(x_prompt.reshape(batch * ctx_len, d), mod[:1], None, batch, ctx_len, ctx_len,
                           wts, emit_state=True)
    s0 = _pair_layout(jnp.swapaxes(state_rwkv[:, l].astype(F32), -1, -2))
    y_lat, _ = _layer(x_sample.reshape(dec_batch * dec_seq, d), mod[1:], s0, dec_batch, dec_seq, GRID_W,
                      wts, emit_state=False)

    new_state = jnp.stack(st_ctx, axis=1)[:, None].astype(x_prompt.dtype)
    return (y_ctx.reshape(batch, ctx_len, d), y_lat.reshape(dec_batch, dec_seq, d), new_state)
```

```python
import functools
import math

import numpy as np
import jax
import jax.numpy as jnp
from jax import lax
from jax.experimental import pallas as pl
from jax.experimental.pallas import tpu as pltpu

F32 = jnp.float32
BF16 = jnp.bfloat16

HEAD = 64
LANES = 128
MXU_COLS = 256
FFN_SUB_ROWS = 256
CHUNK = 64
SCAN_TILE = 256
SCAN_PAIRS = 8
GRID_W = 64
N_GROUPS_B = 4
RMS_EPS = 1e-6
GN_EPS = 64e-5
VMEM_LIMIT = 56 * 1024 * 1024

NN = ((1,), (0,))
NT = ((1,), (1,))
TN = ((0,), (0,))


def _cparams(sem):
    return pltpu.CompilerParams(dimension_semantics=sem, vmem_limit_bytes=VMEM_LIMIT)


def _dot(a, b, dims=NN):
    return lax.dot_general(a, b, (dims, ((), ())), preferred_element_type=F32)


def _split2(x):
    hi = x.astype(BF16)
    lo = (x - hi.astype(F32)).astype(BF16)
    return hi, lo


def _split3(x):
    hi = x.astype(BF16)
    r1 = x - hi.astype(F32)
    mid = r1.astype(BF16)
    lo = (r1 - mid.astype(F32)).astype(BF16)
    return hi, mid, lo


def _dot_split_rhs(a, b, dims=NN):
    ab = a.astype(BF16)
    bh, bl = _split2(b)
    return _dot(ab, bh, dims) + _dot(ab, bl, dims)


def _dot_exact_rhs(a, b_bf16, dims=NN):
    ah, al = _split2(a)
    return _dot(ah, b_bf16, dims) + _dot(al, b_bf16, dims)


def _sigmoid(x):
    return 1.0 / (1.0 + jnp.exp(-x))


def _mod_kernel(c_ref, w_ref, b_ref, o_ref):
    c = c_ref[...]
    s = c * _sigmoid(c)
    o_ref[...] = jnp.dot(s, w_ref[...], preferred_element_type=F32,
                         precision=lax.Precision.HIGHEST) + b_ref[...]


def _modulation(cvec, ada_w, ada_b):
    d, n = ada_w.shape
    tn = 1024 if n % 1024 == 0 else 512
    return pl.pallas_call(
        _mod_kernel,
        grid=(n // tn,),
        in_specs=[pl.BlockSpec((8, d), lambda j: (0, 0)),
                  pl.BlockSpec((d, tn), lambda j: (0, j)),
                  pl.BlockSpec((1, tn), lambda j: (0, j))],
        out_specs=pl.BlockSpec((8, tn), lambda j: (0, j)),
        out_shape=jax.ShapeDtypeStruct((8, n), F32),
        compiler_params=_cparams(("arbitrary",)),
        name="modulation",
    )(cvec, ada_w, ada_b.reshape(1, n))


def _rms_mod(x, g, shift, scale):
    ms = jnp.mean(x * x, axis=-1, keepdims=True)
    y = x * lax.rsqrt(ms + RMS_EPS) * g
    return y * (1.0 + scale) + shift


def _inproj_kernel(x_ref, mod_ref, g_ref, w_ref, o_ref, h_ref):
    @pl.when(pl.program_id(1) == 0)
    def _():
        h = _rms_mod(x_ref[...], g_ref[...], mod_ref[0:1, :], mod_ref[1:2, :])
        h_ref[...] = h.astype(BF16)

    o_ref[...] = jnp.dot(h_ref[...], w_ref[...], preferred_element_type=F32)


def _inproj(x, mod, g, w, rows_per_seg, tm, tn):
    r, d = x.shape
    n = w.shape[1]
    tps = rows_per_seg // tm
    return pl.pallas_call(
        _inproj_kernel,
        grid=(r // tm, n // tn),
        in_specs=[pl.BlockSpec((tm, d), lambda i, j: (i, 0)),
                  pl.BlockSpec((None, 6, d), lambda i, j: (i // tps, 0, 0)),
                  pl.BlockSpec((1, d), lambda i, j: (0, 0)),
                  pl.BlockSpec((d, tn), lambda i, j: (0, j))],
        out_specs=pl.BlockSpec((tm, tn), lambda i, j: (i, j)),
        out_shape=jax.ShapeDtypeStruct((r, n), F32),
        scratch_shapes=[pltpu.VMEM((tm, d), BF16)],
        compiler_params=_cparams(("parallel", "arbitrary")),
        name="inproj",
    )(x, mod, g.reshape(1, d), w)


def _conv3(x, w_ref, b_ref, pos, row_len):
    n = x.shape[0]
    prev = jnp.where(pos == 0, 0.0, pltpu.roll(x, 1, 0))
    nxt = jnp.where(pos == row_len - 1, 0.0, pltpu.roll(x, n - 1, 0))
    return prev * w_ref[0:1, :] + x * w_ref[1:2, :] + nxt * w_ref[2:3, :] + b_ref[...]


def _pair_diag(x, lane_lo):
    return jnp.concatenate([jnp.where(lane_lo, x, 0.0), jnp.where(lane_lo, 0.0, x)], axis=0)


def _scan_kernel(*refs, rev, row_len, has_init, emit_bonus, emit_state, n_pairs):
    (rp_ref, kp_ref, vp_ref, dwda_ref, cwr_ref, cwk_ref, cwv_ref, cbr_ref, cbk_ref, cbv_ref,
     decup_ref, decb_ref, iclup_ref, iclb_ref, kkw_ref, kaw_ref, rkw_ref, ones_ref, tri_ref) = refs[:19]
    rest = list(refs[19:])
    s0_ref = rest.pop(0) if has_init else None
    y_ref = rest.pop(0)
    bonus_ref = rest.pop(0) if emit_bonus else None
    sfin_ref = rest.pop(0) if emit_state else None
    s_ref = rest.pop(0)

    t = pl.program_id(2)
    n_t = pl.num_programs(2)
    tt, width = rp_ref.shape

    @pl.when(t == 0)
    def _():
        if has_init:
            s_ref[...] = s0_ref[...]
        else:
            s_ref[...] = jnp.zeros_like(s_ref)

    ones_bd = ones_ref[...]

    def head_sum(x):
        xb = x.astype(BF16)
        return jnp.concatenate([_dot(xb[:, p * LANES:(p + 1) * LANES], ones_bd) for p in range(n_pairs)], axis=1)

    pos = lax.broadcasted_iota(jnp.int32, (tt, width), 0) & (row_len - 1)
    r = _conv3(rp_ref[...], cwr_ref, cbr_ref, pos, row_len)
    k = _conv3(kp_ref[...], cwk_ref, cbk_ref, pos, row_len)
    v = _conv3(vp_ref[...], cwv_ref, cbv_ref, pos, row_len)

    kk = k * kkw_ref[...]
    kk = kk * lax.rsqrt(head_sum(kk * kk) + 1e-12)
    dwda = dwda_ref[...]
    w_logit = decb_ref[...] + _dot_split_rhs(jnp.tanh(dwda), decup_ref[...])
    logw = (-math.exp(-0.5)) * _sigmoid(w_logit)
    a = _sigmoid(iclb_ref[...] + _dot_split_rhs(dwda, iclup_ref[...]))
    kd = k * (1.0 + (a - 1.0) * kaw_ref[...])
    bb = kk * a
    if emit_bonus:
        bonus_ref[...] = head_sum(r * k * rkw_ref[...]) * v

    tri = tri_ref[...]
    l1, l2, l3 = _split3(logw)
    cum = _dot(tri, l1) + (_dot(tri, l2) + _dot(tri, l3))

    row = lax.broadcasted_iota(jnp.int32, (CHUNK, LANES), 0)
    lane = lax.broadcasted_iota(jnp.int32, (CHUNK, LANES), 1)
    col = lane & (HEAD - 1)
    lane_lo = lane < HEAD
    if rev:
        strict = col > row
        incl = col >= row
    else:
        strict = col < row
        incl = col <= row
    same16 = (row >> 4) == (col >> 4)
    same32 = (row >> 5) == (col >> 5)
    off16 = jnp.logical_and(same32, jnp.logical_not(same16))
    eye = jnp.where(col == row, 1.0, 0.0).astype(F32)
    eye_t = (lax.broadcasted_iota(jnp.int32, (LANES, LANES), 0)
             == lax.broadcasted_iota(jnp.int32, (LANES, LANES), 1)).astype(BF16)

    def b16(x):
        return x.astype(BF16)

    def pd(x):
        return b16(_pair_diag(x, lane_lo))

    def mm(p, q_pd):
        return _dot(b16(p), q_pd)

    n_chunks = tt // CHUNK
    order = list(range(n_chunks - 1, -1, -1) if rev else range(n_chunks))
    last_row = 0 if rev else CHUNK - 1
    streams = [(c, p) for c in order for p in range(n_pairs)]

    q = {}
    for c in order:
        sl = slice(c * CHUNK, (c + 1) * CHUNK)
        cin = cum[sl]
        c_end = cin[last_row:last_row + 1, :]
        g_inv = jnp.exp(-cin)
        g_rat = jnp.exp(c_end - cin)
        rt = r[sl] * jnp.exp(cin)
        kkt = kk[sl] * jnp.exp(cin - logw[sl])
        kt = kd[sl] * g_inv
        bt = bb[sl] * g_inv
        kh = kd[sl] * g_rat
        bh = bb[sl] * g_rat
        vc = v[sl]
        for p in range(n_pairs):
            ls = slice(p * LANES, (p + 1) * LANES)
            bk_t = b16(jnp.concatenate([bh[:, ls], kh[:, ls]], axis=0).T)
            ce_t = jnp.broadcast_to(c_end[:, ls], (LANES, LANES)).T
            g_col = jnp.exp(jnp.where(lane_lo, ce_t[:HEAD], ce_t[HEAD:]))
            q[c, p] = dict(sl=sl, ls=ls, g_col=g_col, rt=b16(rt[:, ls]), kkt=kkt[:, ls], kt=kt[:, ls],
                           bt=bt[:, ls], bk_t=bk_t, vc=vc[:, ls])
    for st in streams:
        d = q[st]
        x = jnp.concatenate([b16(d["kkt"]), d["rt"]], axis=0)
        sbk = _dot(x, jnp.concatenate([pd(d["bt"]), pd(d["kt"])], axis=0), NT)
        sb = sbk[:, :LANES]
        sk = sbk[:, LANES:]
        d["a_b"] = jnp.where(strict, sb[:CHUNK], 0.0)
        d["a_k"] = jnp.where(strict, sk[:CHUNK], 0.0)
        d["m_bk"] = b16(jnp.concatenate([jnp.where(incl, sb[CHUNK:], 0.0), jnp.where(incl, sk[CHUNK:], 0.0)], axis=1))
        d["n0"] = jnp.where(same16, d["a_b"], 0.0)
        d["vpd"] = pd(d["vc"])
    for st in streams:
        d = q[st]
        d["n2"] = mm(d["n0"], pd(d["n0"]))
        d["akv"] = mm(d["a_k"], d["vpd"])
    for st in streams:
        d = q[st]
        t1 = eye - d["n0"]
        both = _dot(b16(jnp.concatenate([d["n2"], t1], axis=0)), pd(d["n2"]))
        d["n4"] = both[:CHUNK]
        d["t"] = t1 + both[CHUNK:]
    for st in streams:
        d = q[st]
        both = _dot(b16(jnp.concatenate([d["n4"], d["t"]], axis=0)), pd(d["n4"]))
        d["n8"] = both[:CHUNK]
        d["t"] = d["t"] + both[CHUNK:]
    for st in streams:
        d = q[st]
        d["t"] = d["t"] + mm(d["t"], pd(d["n8"]))
    for st in streams:
        d = q[st]
        d["p"] = mm(d["t"], pd(jnp.where(off16, d["a_b"], 0.0)))
    for st in streams:
        d = q[st]
        d["t"] = d["t"] - mm(d["p"], pd(d["t"]))
    for st in streams:
        d = q[st]
        d["p"] = mm(d["t"], pd(jnp.where(same32, 0.0, d["a_b"])))
    for st in streams:
        d = q[st]
        d["t"] = d["t"] - mm(d["p"], pd(d["t"]))
    for st in streams:
        d = q[st]
        w12 = _dot(b16(d["t"]), jnp.concatenate([pd(d["kkt"]), pd(d["akv"])], axis=1))
        d["w1r"] = jnp.concatenate([b16(w12[:, :LANES]), d["rt"]], axis=0)
        d["w2"] = w12[:, LANES:]

    s = [s_ref[p] for p in range(n_pairs)]
    for c in order:
        for p in range(n_pairs):
            d = q[c, p]
            xs = _dot(d["w1r"], pd(s[p]))
            d["u"] = -(xs[:CHUNK] + d["w2"])
            d["ys"] = xs[CHUNK:]
            full = _dot(d["bk_t"], b16(jnp.concatenate([d["u"], d["vc"]], axis=0)))
            s[p] = s[p] * d["g_col"] + jnp.where(lane_lo, full[:CHUNK], full[CHUNK:])
    for st in streams:
        d = q[st]
        y_ref[d["sl"], d["ls"]] = d["ys"] + _dot(d["m_bk"], jnp.concatenate([pd(d["u"]), d["vpd"]], axis=0))

    for p in range(n_pairs):
        s_ref[p] = s[p]
    if emit_state:
        @pl.when(t == n_t - 1)
        def _():
            for p in range(n_pairs):
                h1, h2, h3 = _split3(s[p])
                tr = _dot(eye_t, h1, NT) + (_dot(eye_t, h2, NT) + _dot(eye_t, h3, NT))
                sfin_ref[2 * p] = tr[:HEAD]
                sfin_ref[2 * p + 1] = tr[HEAD:]


def _scan(proj, dwda_blk, conv_w, conv_b, dec_up, dec_base, icl_up, icl_base, k_k, k_a, r_k,
          ones_bd, tri, s0, *, batch, seq, d_a, rev, row_len, emit_bonus, emit_state):
    n_t = seq // SCAN_TILE
    n_p = d_a // LANES
    pps = SCAN_PAIRS
    assert n_p % pps == 0
    n_g = n_p // pps
    width = pps * LANES
    tt = SCAN_TILE
    rows = batch * seq

    def tile(b, t):
        return b * n_t + ((n_t - 1 - t) if rev else t)

    def colspec(base):
        return pl.BlockSpec((tt, width), lambda b, g, t, base=base: (tile(b, t), base + g))

    def wspec(nrow, base=0):
        return pl.BlockSpec((nrow, width), lambda b, g, t, base=base: (0, base + g))

    const2 = lambda shape: pl.BlockSpec(shape, lambda b, g, t: (0, 0))
    state_spec = pl.BlockSpec((None, pps, HEAD, LANES), lambda b, g, t: (b, g, 0, 0))
    in_specs = [colspec(0), colspec(n_g), colspec(2 * n_g),
                pl.BlockSpec((tt, LANES), lambda b, g, t: (tile(b, t), dwda_blk)),
                wspec(3, 0), wspec(3, n_g), wspec(3, 2 * n_g),
                wspec(1, 0), wspec(1, n_g), wspec(1, 2 * n_g),
                wspec(LANES), wspec(1), wspec(LANES), wspec(1), wspec(1), wspec(1), wspec(1),
                const2((LANES, LANES)), const2((tt, tt))]
    args = [proj, proj, proj, proj, conv_w, conv_w, conv_w, conv_b, conv_b, conv_b,
            dec_up, dec_base, icl_up, icl_base, k_k, k_a, r_k, ones_bd, tri]
    has_init = s0 is not None
    if has_init:
        in_specs.append(state_spec)
        args.append(s0)
    out_specs = [pl.BlockSpec((tt, width), lambda b, g, t: (tile(b, t), g))]
    out_shape = [jax.ShapeDtypeStruct((rows, d_a), F32)]
    if emit_bonus:
        out_specs.append(pl.BlockSpec((tt, width), lambda b, g, t: (tile(b, t), g)))
        out_shape.append(jax.ShapeDtypeStruct((rows, d_a), F32))
    if emit_state:
        out_specs.append(pl.BlockSpec((None, 2 * pps, HEAD, HEAD), lambda b, g, t: (b, g, 0, 0)))
        out_shape.append(jax.ShapeDtypeStruct((batch, 2 * n_p, HEAD, HEAD), F32))
    kern = functools.partial(_scan_kernel, rev=rev, row_len=row_len, has_init=has_init,
                             emit_bonus=emit_bonus, emit_state=emit_state, n_pairs=pps)
    return pl.pallas_call(
        kern,
        grid=(batch, n_g, n_t),
        in_specs=in_specs,
        out_specs=out_specs,
        out_shape=out_shape,
        scratch_shapes=[pltpu.VMEM((pps, HEAD, LANES), F32)],
        compiler_params=_cparams(("parallel", "parallel", "arbitrary")),
        name="rwkv7_scan_rev" if rev else "rwkv7_scan_fwd",
    )(*args)


def _chan_dft_kernel(x_ref, w_ref, o_ref):
    o_ref[...] = jnp.dot(x_ref[...].astype(BF16), w_ref[...], preferred_element_type=F32).astype(BF16)


def _chan_dft(proj, w_cs, xb_blk, group, tm):
    r = proj.shape[0]
    return pl.pallas_call(
        _chan_dft_kernel,
        grid=(r // tm, N_GROUPS_B),
        in_specs=[pl.BlockSpec((tm, group), lambda i, g: (i, xb_blk + g)),
                  pl.BlockSpec((group, 2 * group), lambda i, g: (0, 0))],
        out_specs=pl.BlockSpec((tm, 2 * group), lambda i, g: (i, g)),
        out_shape=jax.ShapeDtypeStruct((r, N_GROUPS_B * 2 * group), BF16),
        compiler_params=_cparams(("parallel", "arbitrary")),
        name="fourier_channels",
    )(proj, w_cs)


def _time_dft_kernel(ct_ref, st_ref, z_ref, o_ref, *, group):
    zc = z_ref[:, :group]
    zs = z_ref[:, group:]
    y = jnp.dot(ct_ref[...], zc, preferred_element_type=F32) - jnp.dot(st_ref[...], zs, preferred_element_type=F32)
    o_ref[...] = y.astype(BF16)


def _time_dft(z, ct, st, batch, seq, group, tm):
    n_m = seq // tm
    return pl.pallas_call(
        functools.partial(_time_dft_kernel, group=group),
        grid=(batch, N_GROUPS_B, n_m),
        in_specs=[pl.BlockSpec((tm, seq), lambda b, g, m: (m, 0)),
                  pl.BlockSpec((tm, seq), lambda b, g, m: (m, 0)),
                  pl.BlockSpec((seq, 2 * group), lambda b, g, m: (b, g))],
        out_specs=pl.BlockSpec((tm, group), lambda b, g, m: (b * n_m + m, g)),
        out_shape=jax.ShapeDtypeStruct((batch * seq, N_GROUPS_B * group), BF16),
        compiler_params=_cparams(("parallel", "parallel", "arbitrary")),
        name="fourier_positions",
    )(ct, st, z)


def _dft_mats(n):
    scale = 1.0 / math.sqrt(n)
    m = math.isqrt(n)
    if m * m != n or n <= 1024:
        i = lax.broadcasted_iota(jnp.int32, (n, n), 0)
        j = lax.broadcasted_iota(jnp.int32, (n, n), 1)
        ang = ((i * j) % n).astype(F32) * (2.0 * math.pi / n)
        return jnp.cos(ang) * scale, jnp.sin(ang) * scale
    t = lax.broadcasted_iota(jnp.int32, (m, n), 0)
    f = lax.broadcasted_iota(jnp.int32, (m, n), 1)
    a1 = ((m * t * f) % n).astype(F32) * (2.0 * math.pi / n)
    a2 = ((t * f) % n).astype(F32) * (2.0 * math.pi / n)
    c1, s1 = (jnp.cos(a1) * scale)[:, None, :], (jnp.sin(a1) * scale)[:, None, :]
    c2, s2 = jnp.cos(a2)[None, :, :], jnp.sin(a2)[None, :, :]
    return (c1 * c2 - s1 * s2).reshape(n, n), (s1 * c2 + c1 * s2).reshape(n, n)


def _post_kernel(yf_ref, yb_ref, bonus_ref, dg_ref, gup_ref, lg_ref, lb_ref, avg_ref, o_ref):
    avg = avg_ref[...]
    cw = avg.shape[0]
    sig = _sigmoid(dg_ref[...]).astype(BF16)
    for j in range(o_ref.shape[1] // cw):
        cols = slice(j * cw, (j + 1) * cw)
        y = yf_ref[:, cols] + yb_ref[:, cols]
        mu = _dot_exact_rhs(y, avg) * (1.0 / HEAD)
        d = y - mu
        var = _dot_exact_rhs(d * d, avg) * (1.0 / HEAD)
        yn = d * lax.rsqrt(var + GN_EPS) * lg_ref[:, cols] + lb_ref[:, cols]
        g = jnp.dot(sig, gup_ref[:, cols], preferred_element_type=F32)
        o_ref[:, cols] = ((yn + bonus_ref[:, cols]) * g).astype(BF16)


def _post(yf, yb, bonus, proj, dg_blk, gate_up, lnx_g, lnx_b, avg, tm):
    r, d_a = yf.shape
    lg = gate_up.shape[0]
    cw = avg.shape[0]
    tile = pl.BlockSpec((tm, d_a), lambda i: (i, 0))
    vec = pl.BlockSpec((1, d_a), lambda i: (0, 0))
    return pl.pallas_call(
        _post_kernel,
        grid=(r // tm,),
        in_specs=[tile, tile, tile,
                  pl.BlockSpec((tm, lg), lambda i: (i, dg_blk)),
                  pl.BlockSpec((lg, d_a), lambda i: (0, 0)),
                  vec, vec,
                  pl.BlockSpec((cw, cw), lambda i: (0, 0))],
        out_specs=tile,
        out_shape=jax.ShapeDtypeStruct((r, d_a), BF16),
        compiler_params=_cparams(("parallel",)),
        name="headnorm_gate",
    )(yf, yb, bonus, proj, gate_up, lnx_g.reshape(1, d_a), lnx_b.reshape(1, d_a), avg)


def _mix_out_kernel(ya_ref, yb_ref, ga0_ref, ga1_ref, gb0_ref, gb1_ref, wa_ref, wf_ref, wo_ref, x_ref, mod_ref,
                    o_ref):
    half = ga0_ref.shape[1]
    ya = ya_ref[...]
    yb = yb_ref[...]
    parts = []
    for q, (ga_ref, gb_ref) in enumerate(((ga0_ref, gb0_ref), (ga1_ref, gb1_ref))):
        cols = slice(q * half, (q + 1) * half)
        pa = jnp.dot(ya, wa_ref[:, cols], preferred_element_type=F32)
        pb = jnp.dot(yb, wf_ref[:, cols], preferred_element_type=F32)
        parts.append((_sigmoid(ga_ref[...]) * pa + _sigmoid(gb_ref[...]) * pb).astype(BF16))
    mixed = jnp.concatenate(parts, axis=1)
    p = jnp.dot(mixed, wo_ref[...], preferred_element_type=F32)
    o_ref[...] = x_ref[...] + mod_ref[2:3, :] * p


def _mix_out(ya, yb, proj, c_gates, w_a, w_f, w_o, x, mod, rows_per_seg, tm):
    r, d = x.shape
    d_a = ya.shape[1]
    d_b = yb.shape[1]
    half = d // 2
    assert c_gates % half == 0
    gblk = c_gates // half
    tps = rows_per_seg // tm
    resident = lambda shape: pl.BlockSpec(shape, lambda i: (0, 0), pipeline_mode=pl.Buffered(1))
    gate = lambda k: pl.BlockSpec((tm, half), lambda i, k=k: (i, gblk + k))
    return pl.pallas_call(
        _mix_out_kernel,
        grid=(r // tm,),
        in_specs=[pl.BlockSpec((tm, d_a), lambda i: (i, 0)),
                  pl.BlockSpec((tm, d_b), lambda i: (i, 0)),
                  gate(0), gate(1), gate(2), gate(3),
                  resident((d_a, d)), resident((d_b, d)), resident((d, d)),
                  pl.BlockSpec((tm, d), lambda i: (i, 0)),
                  pl.BlockSpec((None, 6, d), lambda i: (i // tps, 0, 0))],
        out_specs=pl.BlockSpec((tm, d), lambda i: (i, 0)),
        out_shape=jax.ShapeDtypeStruct((r, d), F32),
        compiler_params=_cparams(("parallel",)),
        name="merge_outproj",
    )(ya, yb, proj, proj, proj, proj, w_a, w_f, w_o, x, mod)


def _ffn_in_kernel(x_ref, mod_ref, g_ref, wg_ref, wv_ref, cwg_ref, cwv_ref, cbg_ref, cbv_ref, o_ref, h_ref,
                   *, row_len):
    @pl.when(pl.program_id(1) == 0)
    def _():
        h = _rms_mod(x_ref[...], g_ref[...], mod_ref[3:4, :], mod_ref[4:5, :])
        h_ref[...] = h.astype(BF16)

    tm, tn = o_ref.shape
    sub_n = MXU_COLS
    sub_m = min(tm, FFN_SUB_ROWS)
    assert sub_m % row_len == 0
    pos = lax.broadcasted_iota(jnp.int32, (sub_m, sub_n), 0) & (row_len - 1)
    blocks = [(slice(m * sub_m, (m + 1) * sub_m), slice(q * sub_n, (q + 1) * sub_n))
              for m in range(tm // sub_m) for q in range(tn // sub_n)]

    def matmuls(blk):
        rows, cols = blk
        h = h_ref[rows, :]
        return (jnp.dot(h, wg_ref[:, cols], preferred_element_type=F32),
                jnp.dot(h, wv_ref[:, cols], preferred_element_type=F32))

    def epilogue(blk, ug, uv):
        rows, cols = blk
        ug = _conv3(ug, cwg_ref.at[:, cols], cbg_ref.at[:, cols], pos, row_len)
        uv = _conv3(uv, cwv_ref.at[:, cols], cbv_ref.at[:, cols], pos, row_len)
        o_ref[rows, cols] = (ug * _sigmoid(ug) * uv).astype(BF16)

    cur = matmuls(blocks[0])
    for i, blk in enumerate(blocks):
        nxt = matmuls(blocks[i + 1]) if i + 1 < len(blocks) else None
        epilogue(blk, *cur)
        cur = nxt


def _ffn_in(x, mod, g, w2, cw2, cb2, rows_per_seg, row_len, tm, tn):
    r, d = x.shape
    ffp = w2.shape[1] // 2
    nj = ffp // tn
    tps = rows_per_seg // tm
    return pl.pallas_call(
        functools.partial(_ffn_in_kernel, row_len=row_len),
        grid=(r // tm, nj),
        in_specs=[pl.BlockSpec((tm, d), lambda i, j: (i, 0)),
                  pl.BlockSpec((None, 6, d), lambda i, j: (i // tps, 0, 0)),
                  pl.BlockSpec((1, d), lambda i, j: (0, 0)),
                  pl.BlockSpec((d, tn), lambda i, j: (0, j)),
                  pl.BlockSpec((d, tn), lambda i, j: (0, nj + j)),
                  pl.BlockSpec((3, tn), lambda i, j: (0, j)),
                  pl.BlockSpec((3, tn), lambda i, j: (0, nj + j)),
                  pl.BlockSpec((1, tn), lambda i, j: (0, j)),
                  pl.BlockSpec((1, tn), lambda i, j: (0, nj + j))],
        out_specs=pl.BlockSpec((tm, tn), lambda i, j: (i, j)),
        out_shape=jax.ShapeDtypeStruct((r, ffp), BF16),
        scratch_shapes=[pltpu.VMEM((tm, d), BF16)],
        compiler_params=_cparams(("parallel", "arbitrary")),
        name="ffn_in_conv_gate",
    )(x, mod, g.reshape(1, d), w2, w2, cw2, cw2, cb2, cb2)


def _ffn_down_kernel(a_ref, w_ref, x_ref, mod_ref, g_ref, o_ref):
    x2 = x_ref[...] + mod_ref[5:6, :] * jnp.dot(a_ref[...], w_ref[...], preferred_element_type=F32)
    ms = jnp.mean(x2 * x2, axis=-1, keepdims=True)
    o_ref[...] = x2 * lax.rsqrt(ms + RMS_EPS) * g_ref[...]


def _ffn_down(act, w, x, mod, g, rows_per_seg, tm):
    r, d = x.shape
    ffp = act.shape[1]
    tps = rows_per_seg // tm
    return pl.pallas_call(
        _ffn_down_kernel,
        grid=(r // tm,),
        in_specs=[pl.BlockSpec((tm, ffp), lambda i: (i, 0)),
                  pl.BlockSpec((ffp, d), lambda i: (0, 0), pipeline_mode=pl.Buffered(1)),
                  pl.BlockSpec((tm, d), lambda i: (i, 0)),
                  pl.BlockSpec((None, 6, d), lambda i: (i // tps, 0, 0)),
                  pl.BlockSpec((1, d), lambda i: (0, 0))],
        out_specs=pl.BlockSpec((tm, d), lambda i: (i, 0)),
        out_shape=jax.ShapeDtypeStruct((r, d), F32),
        compiler_params=_cparams(("parallel",)),
        name="ffn_down_final_norm",
    )(act, w, x, mod, g.reshape(1, d))


def _pick(n, prefs):
    for p in prefs:
        if n % p == 0:
            return p
    raise ValueError(f"no tile in {prefs} divides {n}")


def _pair_layout(s):
    lead = s.shape[:-3]
    h = s.shape[-3]
    s = s.reshape(lead + (h // 2, 2, HEAD, HEAD))
    s = jnp.swapaxes(s, -3, -2)
    return s.reshape(lead + (h // 2, HEAD, 2 * HEAD))


def _layer(x, mod, s0, batch, seq, row_len, wts, emit_state):
    d = x.shape[1]
    rows = batch * seq
    rows_per_seg = rows // mod.shape[0]
    d_a = wts["d_a"]
    d_b = wts["d_b"]
    group = d_b // N_GROUPS_B
    lora_w = wts["lora_w"]
    lora_g = wts["lora_g"]
    c_xb = 3 * d_a
    c_gates = c_xb + d_b
    c_dw = c_gates + 2 * d
    c_dg = c_dw + LANES
    assert lora_w == HEAD and wts["lora_a"] == HEAD and lora_g == LANES
    assert seq % SCAN_TILE == 0 and SCAN_TILE % row_len == 0 and row_len & (row_len - 1) == 0

    tm = _pick(rows_per_seg, (1024, 512, 256))
    n_in = wts["w_in"].shape[1]
    proj = _inproj(x, mod, wts["norm_mix_g"], wts["w_in"], rows_per_seg, tm, _pick(n_in, (1280, 768, 512, 256)))

    ys, bonus, states = [], None, []
    for dr in range(2):
        outs = _scan(proj, c_dw // LANES, wts["rkv_conv_w"], wts["rkv_conv_b"],
                     wts["decay_up"][dr], wts["decay_base"][dr], wts["iclr_up"][dr], wts["iclr_base"][dr],
                     wts["k_k"], wts["k_a"], wts["r_k"], wts["ones_bd"], wts["tri"][dr],
                     None if s0 is None else s0[:, dr],
                     batch=batch, seq=seq, d_a=d_a, rev=(dr == 1), row_len=row_len,
                     emit_bonus=(dr == 0), emit_state=emit_state)
        ys.append(outs[0])
        if dr == 0:
            bonus = outs[1]
        if emit_state:
            states.append(outs[-1])

    ya = _post(ys[0], ys[1], bonus, proj, c_dg // lora_g, wts["gate_up"], wts["lnx_g"], wts["lnx_b"],
               wts["avg_bd"], 256)

    z = _chan_dft(proj, wts["chan_cs"], c_xb // group, group, _pick(rows, (512, 256)))
    ct, st = wts["time_cs"][seq]
    yb = _time_dft(z, ct, st, batch, seq, group, _pick(seq, (512, 256)))

    x1 = _mix_out(ya, yb, proj, c_gates, wts["w_out_a"], wts["w_fourier"], wts["w_out"], x, mod,
                  rows_per_seg, 256)

    act = _ffn_in(x1, mod, wts["norm_ffn_g"], wts["ffn_w_in"], wts["ffn_conv_w"], wts["ffn_conv_b"],
                  rows_per_seg, row_len, tm, 512)
    y = _ffn_down(act, wts["ffn_w_down"], x1, mod, wts["final_norm_g"], rows_per_seg, 256)
    return y, states


def _tri_blockdiag(n, rev):
    i = np.arange(n)[:, None]
    j = np.arange(n)[None, :]
    same = (i // CHUNK) == (j // CHUNK)
    m = same & ((j >= i) if rev else (j <= i))
    return jnp.asarray(m.astype(np.float32), dtype=BF16)


def _blockdiag_ones(n):
    i = np.arange(n)[:, None] // HEAD
    j = np.arange(n)[None, :] // HEAD
    return jnp.asarray((i == j).astype(np.float32), dtype=BF16)


def kernel(x_prompt, x_sample, state_rwkv, c, c_ctx, ada_w, ada_b, norm_mix_g, w_in, rkv_conv_w, rkv_conv_b,
           decay_up, decay_base, iclr_up, iclr_base, gate_up, k_k, k_a, r_k, lnx_g, lnx_b, w_out_a, w_fourier,
           w_out, norm_ffn_g, ffn_w_in, ffn_conv_w, ffn_conv_b, ffn_w_down, final_norm_g):
    batch, ctx_len, d = x_prompt.shape
    dec_batch, dec_seq, _ = x_sample.shape
    depth = ada_w.shape[0]
    assert depth == 1
    d_a = w_out_a.shape[1]
    d_b = w_fourier.shape[1]
    n_heads = d_a // HEAD
    d_ff = ffn_w_down.shape[1]
    ffp = -(-d_ff // 512) * 512
    lora_w = decay_up.shape[2]
    lora_a = iclr_up.shape[2]
    lora_g = gate_up.shape[1]
    group = d_b // N_GROUPS_B

    cvec = jnp.concatenate([c_ctx[None, :], c], axis=0).astype(F32)
    n_vec = cvec.shape[0]
    cvec = jnp.pad(cvec, ((0, 8 - n_vec), (0, 0)))
    mod = _modulation(cvec, ada_w[0].astype(F32), ada_b[0].astype(F32))[:n_vec].reshape(n_vec, 6, d)

    l = 0
    zpad_w = jnp.zeros((2, LANES - lora_w, d_a), F32)
    zpad_a = jnp.zeros((2, LANES - lora_a, d_a), F32)
    ffn_in = ffn_w_in[l]
    pad_ff = ((0, 0), (0, ffp - d_ff))
    cc, sc = _dft_mats(group)
    wts = {
        "d_a": d_a, "d_b": d_b, "lora_w": lora_w, "lora_a": lora_a, "lora_g": lora_g,
        "norm_mix_g": norm_mix_g[l], "w_in": w_in[l].astype(BF16),
        "rkv_conv_w": rkv_conv_w[l], "rkv_conv_b": rkv_conv_b[l].reshape(1, -1),
        "decay_up": jnp.concatenate([decay_up[l], zpad_w], axis=1),
        "iclr_up": jnp.concatenate([zpad_a, iclr_up[l]], axis=1),
        "decay_base": decay_base[l].reshape(2, 1, d_a), "iclr_base": iclr_base[l].reshape(2, 1, d_a),
        "k_k": k_k[l].reshape(1, d_a), "k_a": k_a[l].reshape(1, d_a), "r_k": r_k[l].reshape(1, d_a),
        "gate_up": gate_up[l].astype(BF16), "lnx_g": lnx_g[l], "lnx_b": lnx_b[l],
        "w_out_a": w_out_a[l].astype(BF16), "w_fourier": w_fourier[l].astype(BF16), "w_out": w_out[l].astype(BF16),
        "norm_ffn_g": norm_ffn_g[l],
        "ffn_w_in": jnp.concatenate([jnp.pad(ffn_in[:, :d_ff], pad_ff), jnp.pad(ffn_in[:, d_ff:], pad_ff)],
                                    axis=1).astype(BF16),
        "ffn_conv_w": jnp.concatenate([jnp.pad(ffn_conv_w[l][:, :d_ff], pad_ff),
                                       jnp.pad(ffn_conv_w[l][:, d_ff:], pad_ff)], axis=1),
        "ffn_conv_b": jnp.concatenate([jnp.pad(ffn_conv_b[l][None, :d_ff], pad_ff),
                                       jnp.pad(ffn_conv_b[l][None, d_ff:], pad_ff)], axis=1),
        "ffn_w_down": jnp.pad(ffn_w_down[l], ((0, ffp - d_ff), (0, 0))).astype(BF16),
        "final_norm_g": final_norm_g,
        "ones_bd": _blockdiag_ones(LANES), "avg_bd": _blockdiag_ones(256),
        "tri": (_tri_blockdiag(SCAN_TILE, False), _tri_blockdiag(SCAN_TILE, True)),
        "chan_cs": jnp.concatenate([cc, sc], axis=1).astype(BF16),
        "time_cs": {n: tuple(m.astype(BF16) for m in _dft_mats(n)) for n in {ctx_len, dec_seq}},
    }

    y_ctx, st_ctx = _layer(x_prompt.reshape(batch * ctx_len, d), mod[:1], None, batch, ctx_len, ctx_len,
                           wts, emit_state=True)
    s0 = _pair_layout(jnp.swapaxes(state_rwkv[:, l].astype(F32), -1, -2))
    y_lat, _ = _layer(x_sample.reshape(dec_batch * dec_seq, d), mod[1:], s0, dec_batch, dec_seq, GRID_W,
                      wts, emit_state=False)

    new_state = jnp.stack(st_ctx, axis=1)[:, None].astype(x_prompt.dtype)
    return (y_ctx.reshape(batch, ctx_len, d), y_lat.reshape(dec_batch, dec_seq, d), new_state)
```

```python
import functools
import math

import numpy as np
import jax
import jax.numpy as jnp
from jax import lax
from jax.experimental import pallas as pl
from jax.experimental.pallas import tpu as pltpu

F32 = jnp.float32
BF16 = jnp.bfloat16

HEAD = 64
LANES = 128
MXU_COLS = 256
FFN_SUB_ROWS = 256
CHUNK = 64
SCAN_TILE = 256
SCAN_PAIRS = 8
GRID_W = 64
N_GROUPS_B = 4
RMS_EPS = 1e-6
GN_EPS = 64e-5
VMEM_LIMIT = 56 * 1024 * 1024

NN = ((1,), (0,))
NT = ((1,), (1,))
TN = ((0,), (0,))


def _cparams(sem):
    return pltpu.CompilerParams(dimension_semantics=sem, vmem_limit_bytes=VMEM_LIMIT)


def _dot(a, b, dims=NN):
    return lax.dot_general(a, b, (dims, ((), ())), preferred_element_type=F32)


def _split2(x):
    hi = x.astype(BF16)
    lo = (x - hi.astype(F32)).astype(BF16)
    return hi, lo


def _split3(x):
    hi = x.astype(BF16)
    r1 = x - hi.astype(F32)
    mid = r1.astype(BF16)
    lo = (r1 - mid.astype(F32)).astype(BF16)
    return hi, mid, lo


def _dot_split_rhs(a, b, dims=NN):
    ab = a.astype(BF16)
    bh, bl = _split2(b)
    return _dot(ab, bh, dims) + _dot(ab, bl, dims)


def _dot_exact_rhs(a, b_bf16, dims=NN):
    ah, al = _split2(a)
    return _dot(ah, b_bf16, dims) + _dot(al, b_bf16, dims)


def _sigmoid(x):
    return 1.0 / (1.0 + jnp.exp(-x))


def _mod_kernel(c_ref, w_ref, b_ref, o_ref):
    c = c_ref[...]
    s = c * _sigmoid(c)
    o_ref[...] = jnp.dot(s, w_ref[...], preferred_element_type=F32,
                         precision=lax.Precision.HIGHEST) + b_ref[...]


def _modulation(cvec, ada_w, ada_b):
    d, n = ada_w.shape
    tn = 1024 if n % 1024 == 0 else 512
    return pl.pallas_call(
        _mod_kernel,
        grid=(n // tn,),
        in_specs=[pl.BlockSpec((8, d), lambda j: (0, 0)),
                  pl.BlockSpec((d, tn), lambda j: (0, j)),
                  pl.BlockSpec((1, tn), lambda j: (0, j))],
        out_specs=pl.BlockSpec((8, tn), lambda j: (0, j)),
        out_shape=jax.ShapeDtypeStruct((8, n), F32),
        compiler_params=_cparams(("arbitrary",)),
        name="modulation",
    )(cvec, ada_w, ada_b.reshape(1, n))


def _rms_mod(x, g, shift, scale):
    ms = jnp.mean(x * x, axis=-1, keepdims=True)
    y = x * lax.rsqrt(ms + RMS_EPS) * g
    return y * (1.0 + scale) + shift


def _inproj_kernel(x_ref, mod_ref, g_ref, w_ref, o_ref, h_ref):
    @pl.when(pl.program_id(1) == 0)
    def _():
        h = _rms_mod(x_ref[...], g_ref[...], mod_ref[0:1, :], mod_ref[1:2, :])
        h_ref[...] = h.astype(BF16)

    o_ref[...] = jnp.dot(h_ref[...], w_ref[...], preferred_element_type=F32)


def _inproj(x, mod, g, w, rows_per_seg, tm, tn):
    r, d = x.shape
    n = w.shape[1]
    tps = rows_per_seg // tm
    return pl.pallas_call(
        _inproj_kernel,
        grid=(r // tm, n // tn),
        in_specs=[pl.BlockSpec((tm, d), lambda i, j: (i, 0)),
                  pl.BlockSpec((None, 6, d), lambda i, j: (i // tps, 0, 0)),
                  pl.BlockSpec((1, d), lambda i, j: (0, 0)),
                  pl.BlockSpec((d, tn), lambda i, j: (0, j))],
        out_specs=pl.BlockSpec((tm, tn), lambda i, j: (i, j)),
        out_shape=jax.ShapeDtypeStruct((r, n), F32),
        scratch_shapes=[pltpu.VMEM((tm, d), BF16)],
        compiler_params=_cparams(("parallel", "arbitrary")),
        name="inproj",
    )(x, mod, g.reshape(1, d), w)


def _conv3(x, w_ref, b_ref, pos, row_len):
    n = x.shape[0]
    prev = jnp.where(pos == 0, 0.0, pltpu.roll(x, 1, 0))
    nxt = jnp.where(pos == row_len - 1, 0.0, pltpu.roll(x, n - 1, 0))
    return prev * w_ref[0:1, :] + x * w_ref[1:2, :] + nxt * w_ref[2:3, :] + b_ref[...]


def _pair_diag(x, lane_lo):
    return jnp.concatenate([jnp.where(lane_lo, x, 0.0), jnp.where(lane_lo, 0.0, x)], axis=0)


def _scan_kernel_two_groups(*refs, rev, row_len, has_init, emit_bonus, emit_state, n_pairs):
    (rp_ref, kp_ref, vp_ref, dwda_ref, cwr_ref, cwk_ref, cwv_ref, cbr_ref, cbk_ref, cbv_ref,
     decup_ref, decb_ref, iclup_ref, iclb_ref, kkw_ref, kaw_ref, rkw_ref, ones_ref, tri_ref) = refs[:19]
    rest = list(refs[19:])
    s0_ref = rest.pop(0) if has_init else None
    y_ref = rest.pop(0)
    bonus_ref = rest.pop(0) if emit_bonus else None
    sfin_ref = rest.pop(0) if emit_state else None
    s_ref = rest.pop(0)

    t = pl.program_id(2)
    n_t = pl.num_programs(2)
    tt = rp_ref.shape[0]

    @pl.when(t == 0)
    def _():
        if has_init:
            s_ref[...] = s0_ref[...]
        else:
            s_ref[...] = jnp.zeros_like(s_ref)

    ones_bd = ones_ref[...]
    tri = tri_ref[...]
    pos = lax.broadcasted_iota(jnp.int32, (tt, LANES), 0) & (row_len - 1)
    row = lax.broadcasted_iota(jnp.int32, (CHUNK, LANES), 0)
    lane = lax.broadcasted_iota(jnp.int32, (CHUNK, LANES), 1)
    col = lane & (HEAD - 1)
    lane_lo = lane < HEAD
    if rev:
        strict = col > row
        incl = col >= row
    else:
        strict = col < row
        incl = col <= row
    same16 = (row >> 4) == (col >> 4)
    same32 = (row >> 5) == (col >> 5)
    off16 = jnp.logical_and(same32, jnp.logical_not(same16))
    eye = jnp.where(col == row, 1.0, 0.0).astype(F32)
    eye_t = (lax.broadcasted_iota(jnp.int32, (LANES, LANES), 0)
             == lax.broadcasted_iota(jnp.int32, (LANES, LANES), 1)).astype(BF16)

    def b16(x):
        return x.astype(BF16)

    def pd(x):
        return b16(_pair_diag(x, lane_lo))

    def mm(p, q_pd):
        return _dot(b16(p), q_pd)

    def head_sum(x):
        return _dot(b16(x), ones_bd)

    n_chunks = tt // CHUNK
    order = list(range(n_chunks - 1, -1, -1) if rev else range(n_chunks))
    last_row = 0 if rev else CHUNK - 1
    q = {}

    def prep_pair(p):
        ls = slice(p * LANES, (p + 1) * LANES)
        r = _conv3(rp_ref[:, ls], cwr_ref.at[:, ls], cbr_ref.at[:, ls], pos, row_len)
        k = _conv3(kp_ref[:, ls], cwk_ref.at[:, ls], cbk_ref.at[:, ls], pos, row_len)
        v = _conv3(vp_ref[:, ls], cwv_ref.at[:, ls], cbv_ref.at[:, ls], pos, row_len)
        kk = k * kkw_ref[:, ls]
        kk = kk * lax.rsqrt(head_sum(kk * kk) + 1e-12)
        dwda = dwda_ref[...]
        w_logit = decb_ref[:, ls] + _dot_split_rhs(jnp.tanh(dwda), decup_ref[:, ls])
        logw = (-math.exp(-0.5)) * _sigmoid(w_logit)
        a = _sigmoid(iclb_ref[:, ls] + _dot_split_rhs(dwda, iclup_ref[:, ls]))
        kd = k * (1.0 + (a - 1.0) * kaw_ref[:, ls])
        bb = kk * a
        if emit_bonus:
            bonus_ref[:, ls] = head_sum(r * k * rkw_ref[:, ls]) * v
        l1, l2, l3 = _split3(logw)
        cum = _dot(tri, l1) + (_dot(tri, l2) + _dot(tri, l3))
        for c in order:
            sl = slice(c * CHUNK, (c + 1) * CHUNK)
            cin = cum[sl]
            c_end = cin[last_row:last_row + 1, :]
            g_inv = jnp.exp(-cin)
            g_rat = jnp.exp(c_end - cin)
            bk_t = b16(jnp.concatenate([bb[sl] * g_rat, kd[sl] * g_rat], axis=0).T)
            ce_t = jnp.broadcast_to(c_end, (LANES, LANES)).T
            g_col = jnp.exp(jnp.where(lane_lo, ce_t[:HEAD], ce_t[HEAD:]))
            q[c, p] = dict(sl=sl, ls=ls, g_col=g_col, rt=b16(r[sl] * jnp.exp(cin)),
                           kkt=kk[sl] * jnp.exp(cin - logw[sl]), kt=kd[sl] * g_inv, bt=bb[sl] * g_inv,
                           bk_t=bk_t, vc=v[sl])

    def stages(pairs):
        streams = [(c, p) for c in order for p in pairs]

        def per_stream(fn):
            def run():
                for st in streams:
                    fn(q[st])
            return run

        def s_scores(d):
            x = jnp.concatenate([b16(d["kkt"]), d["rt"]], axis=0)
            sbk = _dot(x, jnp.concatenate([pd(d["bt"]), pd(d["kt"])], axis=0), NT)
            sb = sbk[:, :LANES]
            sk = sbk[:, LANES:]
            d["a_b"] = jnp.where(strict, sb[:CHUNK], 0.0)
            d["a_k"] = jnp.where(strict, sk[:CHUNK], 0.0)
            d["m_bk"] = b16(jnp.concatenate([jnp.where(incl, sb[CHUNK:], 0.0),
                                             jnp.where(incl, sk[CHUNK:], 0.0)], axis=1))
            d["n0"] = jnp.where(same16, d["a_b"], 0.0)
            d["vpd"] = pd(d["vc"])

        def s_n2(d):
            d["n2"] = mm(d["n0"], pd(d["n0"]))
            d["akv"] = mm(d["a_k"], d["vpd"])

        def s_n4(d):
            t1 = eye - d["n0"]
            both = _dot(b16(jnp.concatenate([d["n2"], t1], axis=0)), pd(d["n2"]))
            d["n4"] = both[:CHUNK]
            d["t"] = t1 + both[CHUNK:]

        def s_n8(d):
            both = _dot(b16(jnp.concatenate([d["n4"], d["t"]], axis=0)), pd(d["n4"]))
            d["n8"] = both[:CHUNK]
            d["t"] = d["t"] + both[CHUNK:]

        def s_t16(d):
            d["t"] = d["t"] + mm(d["t"], pd(d["n8"]))

        def s_p32(d):
            d["p"] = mm(d["t"], pd(jnp.where(off16, d["a_b"], 0.0)))

        def s_t(d):
            d["t"] = d["t"] - mm(d["p"], pd(d["t"]))

        def s_p64(d):
            d["p"] = mm(d["t"], pd(jnp.where(same32, 0.0, d["a_b"])))

        def s_w(d):
            w12 = _dot(b16(d["t"]), jnp.concatenate([pd(d["kkt"]), pd(d["akv"])], axis=1))
            d["w1r"] = jnp.concatenate([b16(w12[:, :LANES]), d["rt"]], axis=0)
            d["w2"] = w12[:, LANES:]

        out = [per_stream(f) for f in (s_scores, s_n2, s_n4, s_n8, s_t16, s_p32, s_t, s_p64, s_t, s_w)]

        def chain(c):
            def run():
                for p in pairs:
                    d = q[c, p]
                    xs = _dot(d["w1r"], pd(state[p]))
                    d["u"] = -(xs[:CHUNK] + d["w2"])
                    d["ys"] = xs[CHUNK:]
                    full = _dot(d["bk_t"], b16(jnp.concatenate([d["u"], d["vc"]], axis=0)))
                    state[p] = state[p] * d["g_col"] + jnp.where(lane_lo, full[:CHUNK], full[CHUNK:])
            return run

        out += [chain(c) for c in order]

        def s_y(d):
            y_ref[d["sl"], d["ls"]] = d["ys"] + _dot(d["m_bk"], jnp.concatenate([pd(d["u"]), d["vpd"]], axis=0))

        out.append(per_stream(s_y))
        return out

    state = {p: s_ref[p] for p in range(n_pairs)}
    half = max(n_pairs // 2, 1)
    first, second = list(range(half)), list(range(half, n_pairs))
    for p in first:
        prep_pair(p)
    seq_a = stages(first)
    seq_b = [functools.partial(prep_pair, p) for p in second] + (stages(second) if second else [])
    for i in range(max(len(seq_a), len(seq_b))):
        if i < len(seq_a):
            seq_a[i]()
        if i < len(seq_b):
            seq_b[i]()

    for p in range(n_pairs):
        s_ref[p] = state[p]
    if emit_state:
        @pl.when(t == n_t - 1)
        def _():
            for p in range(n_pairs):
                h1, h2, h3 = _split3(state[p])
                tr = _dot(eye_t, h1, NT) + (_dot(eye_t, h2, NT) + _dot(eye_t, h3, NT))
                sfin_ref[2 * p] = tr[:HEAD]
                sfin_ref[2 * p + 1] = tr[HEAD:]


def _scan_kernel(*refs, rev, row_len, has_init, emit_bonus, emit_state, n_pairs):
    (rp_ref, kp_ref, vp_ref, dwda_ref, cwr_ref, cwk_ref, cwv_ref, cbr_ref, cbk_ref, cbv_ref,
     decup_ref, decb_ref, iclup_ref, iclb_ref, kkw_ref, kaw_ref, rkw_ref, ones_ref, tri_ref) = refs[:19]
    rest = list(refs[19:])
    s0_ref = rest.pop(0) if has_init else None
    y_ref = rest.pop(0)
    bonus_ref = rest.pop(0) if emit_bonus else None
    sfin_ref = rest.pop(0) if emit_state else None
    s_ref = rest.pop(0)

    t = pl.program_id(2)
    n_t = pl.num_programs(2)
    tt, width = rp_ref.shape

    @pl.when(t == 0)
    def _():
        if has_init:
            s_ref[...] = s0_ref[...]
        else:
            s_ref[...] = jnp.zeros_like(s_ref)

    ones_bd = ones_ref[...]

    def head_sum(x):
        xb = x.astype(BF16)
        return jnp.concatenate([_dot(xb[:, p * LANES:(p + 1) * LANES], ones_bd) for p in range(n_pairs)], axis=1)

    pos = lax.broadcasted_iota(jnp.int32, (tt, width), 0) & (row_len - 1)
    r = _conv3(rp_ref[...], cwr_ref, cbr_ref, pos, row_len)
    k = _conv3(kp_ref[...], cwk_ref, cbk_ref, pos, row_len)
    v = _conv3(vp_ref[...], cwv_ref, cbv_ref, pos, row_len)

    kk = k * kkw_ref[...]
    kk = kk * lax.rsqrt(head_sum(kk * kk) + 1e-12)
    dwda = dwda_ref[...]
    w_logit = decb_ref[...] + _dot_split_rhs(jnp.tanh(dwda), decup_ref[...])
    logw = (-math.exp(-0.5)) * _sigmoid(w_logit)
    a = _sigmoid(iclb_ref[...] + _dot_split_rhs(dwda, iclup_ref[...]))
    kd = k * (1.0 + (a - 1.0) * kaw_ref[...])
    bb = kk * a
    if emit_bonus:
        bonus_ref[...] = head_sum(r * k * rkw_ref[...]) * v

    tri = tri_ref[...]
    l1, l2, l3 = _split3(logw)
    cum = _dot(tri, l1) + (_dot(tri, l2) + _dot(tri, l3))

    row = lax.broadcasted_iota(jnp.int32, (CHUNK, LANES), 0)
    lane = lax.broadcasted_iota(jnp.int32, (CHUNK, LANES), 1)
    col = lane & (HEAD - 1)
    lane_lo = lane < HEAD
    if rev:
        strict = col > row
        incl = col >= row
    else:
        strict = col < row
        incl = col <= row
    same16 = (row >> 4) == (col >> 4)
    same32 = (row >> 5) == (col >> 5)
    off16 = jnp.logical_and(same32, jnp.logical_not(same16))
    eye = jnp.where(col == row, 1.0, 0.0).astype(F32)
    eye_t = (lax.broadcasted_iota(jnp.int32, (LANES, LANES), 0)
             == lax.broadcasted_iota(jnp.int32, (LANES, LANES), 1)).astype(BF16)

    def b16(x):
        return x.astype(BF16)

    def pd(x):
        return b16(_pair_diag(x, lane_lo))

    def mm(p, q_pd):
        return _dot(b16(p), q_pd)

    n_chunks = tt // CHUNK
    order = list(range(n_chunks - 1, -1, -1) if rev else range(n_chunks))
    last_row = 0 if rev else CHUNK - 1
    streams = [(c, p) for c in order for p in range(n_pairs)]

    q = {}
    for c in order:
        sl = slice(c * CHUNK, (c + 1) * CHUNK)
        cin = cum[sl]
        c_end = cin[last_row:last_row + 1, :]
        g_inv = jnp.exp(-cin)
        g_rat = jnp.exp(c_end - cin)
        rt = r[sl] * jnp.exp(cin)
        kkt = kk[sl] * jnp.exp(cin - logw[sl])
        kt = kd[sl] * g_inv
        bt = bb[sl] * g_inv
        kh = kd[sl] * g_rat
        bh = bb[sl] * g_rat
        vc = v[sl]
        for p in range(n_pairs):
            ls = slice(p * LANES, (p + 1) * LANES)
            bk_t = b16(jnp.concatenate([bh[:, ls], kh[:, ls]], axis=0).T)
            ce_t = jnp.broadcast_to(c_end[:, ls], (LANES, LANES)).T
            g_col = jnp.exp(jnp.where(lane_lo, ce_t[:HEAD], ce_t[HEAD:]))
            q[c, p] = dict(sl=sl, ls=ls, g_col=g_col, rt=b16(rt[:, ls]), kkt=kkt[:, ls], kt=kt[:, ls],
                           bt=bt[:, ls], bk_t=bk_t, vc=vc[:, ls])
    for st in streams:
        d = q[st]
        x = jnp.concatenate([b16(d["kkt"]), d["rt"]], axis=0)
        sbk = _dot(x, jnp.concatenate([pd(d["bt"]), pd(d["kt"])], axis=0), NT)
        sb = sbk[:, :LANES]
        sk = sbk[:, LANES:]
        d["a_b"] = jnp.where(strict, sb[:CHUNK], 0.0)
        d["a_k"] = jnp.where(strict, sk[:CHUNK], 0.0)
        d["m_bk"] = b16(jnp.concatenate([jnp.where(incl, sb[CHUNK:], 0.0), jnp.where(incl, sk[CHUNK:], 0.0)], axis=1))
        d["n0"] = jnp.where(same16, d["a_b"], 0.0)
        d["vpd"] = pd(d["vc"])
    for st in streams:
        d = q[st]
        d["n2"] = mm(d["n0"], pd(d["n0"]))
        d["akv"] = mm(d["a_k"], d["vpd"])
    for st in streams:
        d = q[st]
        t1 = eye - d["n0"]
        both = _dot(b16(jnp.concatenate([d["n2"], t1], axis=0)), pd(d["n2"]))
        d["n4"] = both[:CHUNK]
        d["t"] = t1 + both[CHUNK:]
    for st in streams:
        d = q[st]
        both = _dot(b16(jnp.concatenate([d["n4"], d["t"]], axis=0)), pd(d["n4"]))
        d["n8"] = both[:CHUNK]
        d["t"] = d["t"] + both[CHUNK:]
    for st in streams:
        d = q[st]
        d["t"] = d["t"] + mm(d["t"], pd(d["n8"]))
    for st in streams:
        d = q[st]
        d["p"] = mm(d["t"], pd(jnp.where(off16, d["a_b"], 0.0)))
    for st in streams:
        d = q[st]
        d["t"] = d["t"] - mm(d["p"], pd(d["t"]))
    for st in streams:
        d = q[st]
        d["p"] = mm(d["t"], pd(jnp.where(same32, 0.0, d["a_b"])))
    for st in streams:
        d = q[st]
        d["t"] = d["t"] - mm(d["p"], pd(d["t"]))
    for st in streams:
        d = q[st]
        w12 = _dot(b16(d["t"]), jnp.concatenate([pd(d["kkt"]), pd(d["akv"])], axis=1))
        d["w1r"] = jnp.concatenate([b16(w12[:, :LANES]), d["rt"]], axis=0)
        d["w2"] = w12[:, LANES:]

    s = [s_ref[p] for p in range(n_pairs)]
    for c in order:
        for p in range(n_pairs):
            d = q[c, p]
            xs = _dot(d["w1r"], pd(s[p]))
            d["u"] = -(xs[:CHUNK] + d["w2"])
            d["ys"] = xs[CHUNK:]
            full = _dot(d["bk_t"], b16(jnp.concatenate([d["u"], d["vc"]], axis=0)))
            s[p] = s[p] * d["g_col"] + jnp.where(lane_lo, full[:CHUNK], full[CHUNK:])
    for st in streams:
        d = q[st]
        y = d["ys"] + _dot(d["m_bk"], jnp.concatenate([pd(d["u"]), d["vpd"]], axis=0))
        y_ref[d["sl"], d["ls"]] = y.astype(y_ref.dtype)

    for p in range(n_pairs):
        s_ref[p] = s[p]
    if emit_state:
        @pl.when(t == n_t - 1)
        def _():
            for p in range(n_pairs):
                h1, h2, h3 = _split3(s[p])
                tr = _dot(eye_t, h1, NT) + (_dot(eye_t, h2, NT) + _dot(eye_t, h3, NT))
                sfin_ref[2 * p] = tr[:HEAD]
                sfin_ref[2 * p + 1] = tr[HEAD:]


def _scan(proj, dwda_blk, conv_w, conv_b, dec_up, dec_base, icl_up, icl_base, k_k, k_a, r_k,
          ones_bd, tri, s0, *, batch, seq, d_a, rev, row_len, emit_bonus, emit_state):
    n_t = seq // SCAN_TILE
    n_p = d_a // LANES
    pps = SCAN_PAIRS
    assert n_p % pps == 0
    n_g = n_p // pps
    width = pps * LANES
    tt = SCAN_TILE
    rows = batch * seq

    def tile(b, t):
        return b * n_t + ((n_t - 1 - t) if rev else t)

    def colspec(base):
        return pl.BlockSpec((tt, width), lambda b, g, t, base=base: (tile(b, t), base + g))

    def wspec(nrow, base=0):
        return pl.BlockSpec((nrow, width), lambda b, g, t, base=base: (0, base + g))

    const2 = lambda shape: pl.BlockSpec(shape, lambda b, g, t: (0, 0))
    state_spec = pl.BlockSpec((None, pps, HEAD, LANES), lambda b, g, t: (b, g, 0, 0))
    in_specs = [colspec(0), colspec(n_g), colspec(2 * n_g),
                pl.BlockSpec((tt, LANES), lambda b, g, t: (tile(b, t), dwda_blk)),
                wspec(3, 0), wspec(3, n_g), wspec(3, 2 * n_g),
                wspec(1, 0), wspec(1, n_g), wspec(1, 2 * n_g),
                wspec(LANES), wspec(1), wspec(LANES), wspec(1), wspec(1), wspec(1), wspec(1),
                const2((LANES, LANES)), const2((tt, tt))]
    args = [proj, proj, proj, proj, conv_w, conv_w, conv_w, conv_b, conv_b, conv_b,
            dec_up, dec_base, icl_up, icl_base, k_k, k_a, r_k, ones_bd, tri]
    has_init = s0 is not None
    if has_init:
        in_specs.append(state_spec)
        args.append(s0)
    out_specs = [pl.BlockSpec((tt, width), lambda b, g, t: (tile(b, t), g))]
    out_shape = [jax.ShapeDtypeStruct((rows, d_a), BF16)]
    if emit_bonus:
        out_specs.append(pl.BlockSpec((tt, width), lambda b, g, t: (tile(b, t), g)))
        out_shape.append(jax.ShapeDtypeStruct((rows, d_a), F32))
    if emit_state:
        out_specs.append(pl.BlockSpec((None, 2 * pps, HEAD, HEAD), lambda b, g, t: (b, g, 0, 0)))
        out_shape.append(jax.ShapeDtypeStruct((batch, 2 * n_p, HEAD, HEAD), F32))
    kern = functools.partial(_scan_kernel, rev=rev, row_len=row_len, has_init=has_init,
                             emit_bonus=emit_bonus, emit_state=emit_state, n_pairs=pps)
    return pl.pallas_call(
        kern,
        grid=(batch, n_g, n_t),
        in_specs=in_specs,
        out_specs=out_specs,
        out_shape=out_shape,
        scratch_shapes=[pltpu.VMEM((pps, HEAD, LANES), F32)],
        compiler_params=_cparams(("parallel", "parallel", "arbitrary")),
        name="rwkv7_scan_rev" if rev else "rwkv7_scan_fwd",
    )(*args)


def _chan_dft_kernel(x_ref, w_ref, o_ref):
    group = w_ref.shape[0]
    w = w_ref[...]
    for g in range(x_ref.shape[1] // group):
        xg = x_ref[:, g * group:(g + 1) * group].astype(BF16)
        o_ref[:, 2 * g * group:2 * (g + 1) * group] = jnp.dot(xg, w, preferred_element_type=F32).astype(BF16)


def _chan_dft(proj, w_cs, xb_blk, d_b, tm):
    r = proj.shape[0]
    group = w_cs.shape[0]
    return pl.pallas_call(
        _chan_dft_kernel,
        grid=(r // tm,),
        in_specs=[pl.BlockSpec((tm, d_b), lambda i: (i, xb_blk)),
                  pl.BlockSpec((group, 2 * group), lambda i: (0, 0))],
        out_specs=pl.BlockSpec((tm, 2 * d_b), lambda i: (i, 0)),
        out_shape=jax.ShapeDtypeStruct((r, 2 * d_b), BF16),
        compiler_params=_cparams(("parallel",)),
        name="fourier_channels",
    )(proj, w_cs)


def _time_dft_kernel(ct_ref, st_ref, z_ref, o_ref):
    ct = ct_ref[...]
    st = st_ref[...]
    group = o_ref.shape[1] // N_GROUPS_B
    for g in range(N_GROUPS_B):
        zc = z_ref[:, 2 * g * group:(2 * g + 1) * group]
        zs = z_ref[:, (2 * g + 1) * group:(2 * g + 2) * group]
        y = jnp.dot(ct, zc, preferred_element_type=F32) - jnp.dot(st, zs, preferred_element_type=F32)
        o_ref[:, g * group:(g + 1) * group] = y.astype(BF16)


def _time_dft(z, ct, st, batch, seq, tm):
    n_m = seq // tm
    d_b = z.shape[1] // 2
    return pl.pallas_call(
        _time_dft_kernel,
        grid=(batch, n_m),
        in_specs=[pl.BlockSpec((tm, seq), lambda b, m: (m, 0)),
                  pl.BlockSpec((tm, seq), lambda b, m: (m, 0)),
                  pl.BlockSpec((seq, 2 * d_b), lambda b, m: (b, 0), pipeline_mode=pl.Buffered(1))],
        out_specs=pl.BlockSpec((tm, d_b), lambda b, m: (b * n_m + m, 0)),
        out_shape=jax.ShapeDtypeStruct((batch * seq, d_b), BF16),
        compiler_params=_cparams(("parallel", "arbitrary")),
        name="fourier_positions",
    )(ct, st, z)


def _dft_mats(n):
    scale = 1.0 / math.sqrt(n)
    m = math.isqrt(n)
    if m * m != n or n <= 1024:
        i = lax.broadcasted_iota(jnp.int32, (n, n), 0)
        j = lax.broadcasted_iota(jnp.int32, (n, n), 1)
        ang = ((i * j) % n).astype(F32) * (2.0 * math.pi / n)
        return jnp.cos(ang) * scale, jnp.sin(ang) * scale
    t = lax.broadcasted_iota(jnp.int32, (m, n), 0)
    f = lax.broadcasted_iota(jnp.int32, (m, n), 1)
    a1 = ((m * t * f) % n).astype(F32) * (2.0 * math.pi / n)
    a2 = ((t * f) % n).astype(F32) * (2.0 * math.pi / n)
    c1, s1 = (jnp.cos(a1) * scale)[:, None, :], (jnp.sin(a1) * scale)[:, None, :]
    c2, s2 = jnp.cos(a2)[None, :, :], jnp.sin(a2)[None, :, :]
    return (c1 * c2 - s1 * s2).reshape(n, n), (s1 * c2 + c1 * s2).reshape(n, n)


def _mix_out_kernel(yf_ref, yr_ref, bonus_ref, dg_ref, gup_ref, lg_ref, lb_ref, avg_ref, yb_ref,
                    ga0_ref, ga1_ref, gb0_ref, gb1_ref, wa_ref, wf_ref, wo_ref, x_ref, mod_ref, o_ref, ya_ref):
    avg = avg_ref[...]
    cw = avg.shape[0]
    sig = _sigmoid(dg_ref[...]).astype(BF16)
    for j in range(ya_ref.shape[1] // cw):
        cols = slice(j * cw, (j + 1) * cw)
        y = yf_ref[:, cols].astype(F32) + yr_ref[:, cols].astype(F32)
        mu = _dot_exact_rhs(y, avg) * (1.0 / HEAD)
        d = y - mu
        var = _dot_exact_rhs(d * d, avg) * (1.0 / HEAD)
        yn = d * lax.rsqrt(var + GN_EPS) * lg_ref[:, cols] + lb_ref[:, cols]
        g = jnp.dot(sig, gup_ref[:, cols], preferred_element_type=F32)
        ya_ref[:, cols] = ((yn + bonus_ref[:, cols]) * g).astype(BF16)

    half = ga0_ref.shape[1]
    ya = ya_ref[...]
    yb = yb_ref[...]
    parts = []
    for q, (ga_ref, gb_ref) in enumerate(((ga0_ref, gb0_ref), (ga1_ref, gb1_ref))):
        cols = slice(q * half, (q + 1) * half)
        pa = jnp.dot(ya, wa_ref[:, cols], preferred_element_type=F32)
        pb = jnp.dot(yb, wf_ref[:, cols], preferred_element_type=F32)
        parts.append((_sigmoid(ga_ref[...]) * pa + _sigmoid(gb_ref[...]) * pb).astype(BF16))
    mixed = jnp.concatenate(parts, axis=1)
    p = jnp.dot(mixed, wo_ref[...], preferred_element_type=F32)
    o_ref[...] = x_ref[...] + mod_ref[2:3, :] * p


def _mix_out(yf, yr, bonus, dg_blk, gate_up, lnx_g, lnx_b, avg, yb, proj, c_gates, w_a, w_f, w_o, x, mod,
             rows_per_seg, tm):
    r, d = x.shape
    d_a = yf.shape[1]
    d_b = yb.shape[1]
    lg = gate_up.shape[0]
    cw = avg.shape[0]
    half = d // 2
    assert c_gates % half == 0
    gblk = c_gates // half
    tps = rows_per_seg // tm
    resident = lambda shape: pl.BlockSpec(shape, lambda i: (0, 0), pipeline_mode=pl.Buffered(1))
    gate = lambda k: pl.BlockSpec((tm, half), lambda i, k=k: (i, gblk + k))
    tile_a = pl.BlockSpec((tm, d_a), lambda i: (i, 0))
    vec_a = pl.BlockSpec((1, d_a), lambda i: (0, 0))
    return pl.pallas_call(
        _mix_out_kernel,
        grid=(r // tm,),
        in_specs=[tile_a, tile_a, tile_a,
                  pl.BlockSpec((tm, lg), lambda i: (i, dg_blk)),
                  resident((lg, d_a)), vec_a, vec_a, resident((cw, cw)),
                  pl.BlockSpec((tm, d_b), lambda i: (i, 0)),
                  gate(0), gate(1), gate(2), gate(3),
                  resident((d_a, d)), resident((d_b, d)), resident((d, d)),
                  pl.BlockSpec((tm, d), lambda i: (i, 0)),
                  pl.BlockSpec((None, 6, d), lambda i: (i // tps, 0, 0))],
        out_specs=pl.BlockSpec((tm, d), lambda i: (i, 0)),
        out_shape=jax.ShapeDtypeStruct((r, d), F32),
        scratch_shapes=[pltpu.VMEM((tm, d_a), BF16)],
        compiler_params=_cparams(("parallel",)),
        name="headnorm_merge_outproj",
    )(yf, yr, bonus, proj, gate_up, lnx_g.reshape(1, d_a), lnx_b.reshape(1, d_a), avg, yb,
      proj, proj, proj, proj, w_a, w_f, w_o, x, mod)


def _ffn_in_kernel(x_ref, mod_ref, g_ref, wg_ref, wv_ref, cwg_ref, cwv_ref, cbg_ref, cbv_ref, o_ref, h_ref,
                   *, row_len):
    @pl.when(pl.program_id(1) == 0)
    def _():
        h = _rms_mod(x_ref[...], g_ref[...], mod_ref[3:4, :], mod_ref[4:5, :])
        h_ref[...] = h.astype(BF16)

    tm, tn = o_ref.shape
    sub_n = MXU_COLS
    sub_m = min(tm, FFN_SUB_ROWS)
    assert sub_m % row_len == 0
    pos = lax.broadcasted_iota(jnp.int32, (sub_m, sub_n), 0) & (row_len - 1)
    blocks = [(slice(m * sub_m, (m + 1) * sub_m), slice(q * sub_n, (q + 1) * sub_n))
              for m in range(tm // sub_m) for q in range(tn // sub_n)]

    def matmuls(blk):
        rows, cols = blk
        h = h_ref[rows, :]
        return (jnp.dot(h, wg_ref[:, cols], preferred_element_type=F32),
                jnp.dot(h, wv_ref[:, cols], preferred_element_type=F32))

    def epilogue(blk, ug, uv):
        rows, cols = blk
        ug = _conv3(ug, cwg_ref.at[:, cols], cbg_ref.at[:, cols], pos, row_len)
        uv = _conv3(uv, cwv_ref.at[:, cols], cbv_ref.at[:, cols], pos, row_len)
        o_ref[rows, cols] = (ug * _sigmoid(ug) * uv).astype(BF16)

    cur = matmuls(blocks[0])
    for i, blk in enumerate(blocks):
        nxt = matmuls(blocks[i + 1]) if i + 1 < len(blocks) else None
        epilogue(blk, *cur)
        cur = nxt


def _ffn_in(x, mod, g, w2, cw2, cb2, rows_per_seg, row_len, tm, tn):
    r, d = x.shape
    ffp = w2.shape[1] // 2
    nj = ffp // tn
    tps = rows_per_seg // tm
    return pl.pallas_call(
        functools.partial(_ffn_in_kernel, row_len=row_len),
        grid=(r // tm, nj),
        in_specs=[pl.BlockSpec((tm, d), lambda i, j: (i, 0)),
                  pl.BlockSpec((None, 6, d), lambda i, j: (i // tps, 0, 0)),
                  pl.BlockSpec((1, d), lambda i, j: (0, 0)),
                  pl.BlockSpec((d, tn), lambda i, j: (0, j)),
                  pl.BlockSpec((d, tn), lambda i, j: (0, nj + j)),
                  pl.BlockSpec((3, tn), lambda i, j: (0, j)),
                  pl.BlockSpec((3, tn), lambda i, j: (0, nj + j)),
                  pl.BlockSpec((1, tn), lambda i, j: (0, j)),
                  pl.BlockSpec((1, tn), lambda i, j: (0, nj + j))],
        out_specs=pl.BlockSpec((tm, tn), lambda i, j: (i, j)),
        out_shape=jax.ShapeDtypeStruct((r, ffp), BF16),
        scratch_shapes=[pltpu.VMEM((tm, d), BF16)],
        compiler_params=_cparams(("parallel", "arbitrary")),
        name="ffn_in_conv_gate",
    )(x, mod, g.reshape(1, d), w2, w2, cw2, cw2, cb2, cb2)


def _ffn_down_kernel(a_ref, w_ref, x_ref, mod_ref, g_ref, o_ref):
    x2 = x_ref[...] + mod_ref[5:6, :] * jnp.dot(a_ref[...], w_ref[...], preferred_element_type=F32)
    ms = jnp.mean(x2 * x2, axis=-1, keepdims=True)
    o_ref[...] = x2 * lax.rsqrt(ms + RMS_EPS) * g_ref[...]


def _ffn_down(act, w, x, mod, g, rows_per_seg, tm):
    r, d = x.shape
    ffp = act.shape[1]
    tps = rows_per_seg // tm
    return pl.pallas_call(
        _ffn_down_kernel,
        grid=(r // tm,),
        in_specs=[pl.BlockSpec((tm, ffp), lambda i: (i, 0)),
                  pl.BlockSpec((ffp, d), lambda i: (0, 0), pipeline_mode=pl.Buffered(1)),
                  pl.BlockSpec((tm, d), lambda i: (i, 0)),
                  pl.BlockSpec((None, 6, d), lambda i: (i // tps, 0, 0)),
                  pl.BlockSpec((1, d), lambda i: (0, 0))],
        out_specs=pl.BlockSpec((tm, d), lambda i: (i, 0)),
        out_shape=jax.ShapeDtypeStruct((r, d), F32),
        compiler_params=_cparams(("parallel",)),
        name="ffn_down_final_norm",
    )(act, w, x, mod, g.reshape(1, d))


def _pick(n, prefs):
    for p in prefs:
        if n % p == 0:
            return p
    raise ValueError(f"no tile in {prefs} divides {n}")


def _pair_layout(s):
    lead = s.shape[:-3]
    h = s.shape[-3]
    s = s.reshape(lead + (h // 2, 2, HEAD, HEAD))
    s = jnp.swapaxes(s, -3, -2)
    return s.reshape(lead + (h // 2, HEAD, 2 * HEAD))


def _layer(x, mod, s0, batch, seq, row_len, wts, emit_state):
    d = x.shape[1]
    rows = batch * seq
    rows_per_seg = rows // mod.shape[0]
    d_a = wts["d_a"]
    d_b = wts["d_b"]
    group = d_b // N_GROUPS_B
    lora_w = wts["lora_w"]
    lora_g = wts["lora_g"]
    c_xb = 3 * d_a
    c_gates = c_xb + d_b
    c_dw = c_gates + 2 * d
    c_dg = c_dw + LANES
    assert lora_w == HEAD and wts["lora_a"] == HEAD and lora_g == LANES
    assert seq % SCAN_TILE == 0 and SCAN_TILE % row_len == 0 and row_len & (row_len - 1) == 0

    tm = _pick(rows_per_seg, (1024, 512, 256))
    n_in = wts["w_in"].shape[1]
    proj = _inproj(x, mod, wts["norm_mix_g"], wts["w_in"], rows_per_seg, tm, _pick(n_in, (1280, 768, 512, 256)))

    ys, bonus, states = [], None, []
    for dr in range(2):
        outs = _scan(proj, c_dw // LANES, wts["rkv_conv_w"], wts["rkv_conv_b"],
                     wts["decay_up"][dr], wts["decay_base"][dr], wts["iclr_up"][dr], wts["iclr_base"][dr],
                     wts["k_k"], wts["k_a"], wts["r_k"], wts["ones_bd"], wts["tri"][dr],
                     None if s0 is None else s0[:, dr],
                     batch=batch, seq=seq, d_a=d_a, rev=(dr == 1), row_len=row_len,
                     emit_bonus=(dr == 0), emit_state=emit_state)
        ys.append(outs[0])
        if dr == 0:
            bonus = outs[1]
        if emit_state:
            states.append(outs[-1])

    assert c_xb % d_b == 0
    z = _chan_dft(proj, wts["chan_cs"], c_xb // d_b, d_b, _pick(rows, (512, 256)))
    ct, st = wts["time_cs"][seq]
    yb = _time_dft(z, ct, st, batch, seq, _pick(seq, (512, 256)))

    x1 = _mix_out(ys[0], ys[1], bonus, c_dg // lora_g, wts["gate_up"], wts["lnx_g"], wts["lnx_b"], wts["avg_bd"],
                  yb, proj, c_gates, wts["w_out_a"], wts["w_fourier"], wts["w_out"], x, mod, rows_per_seg, 256)

    act = _ffn_in(x1, mod, wts["norm_ffn_g"], wts["ffn_w_in"], wts["ffn_conv_w"], wts["ffn_conv_b"],
                  rows_per_seg, row_len, tm, 512)
    y = _ffn_down(act, wts["ffn_w_down"], x1, mod, wts["final_norm_g"], rows_per_seg, 256)
    return y, states


def _tri_blockdiag(n, rev):
    i = np.arange(n)[:, None]
    j = np.arange(n)[None, :]
    same = (i // CHUNK) == (j // CHUNK)
    m = same & ((j >= i) if rev else (j <= i))
    return jnp.asarray(m.astype(np.float32), dtype=BF16)


def _blockdiag_ones(n):
    i = np.arange(n)[:, None] // HEAD
    j = np.arange(n)[None, :] // HEAD
    return jnp.asarray((i == j).astype(np.float32), dtype=BF16)


def kernel(x_prompt, x_sample, state_rwkv, c, c_ctx, ada_w, ada_b, norm_mix_g, w_in, rkv_conv_w, rkv_conv_b,
           decay_up, decay_base, iclr_up, iclr_base, gate_up, k_k, k_a, r_k, lnx_g, lnx_b, w_out_a, w_fourier,
           w_out, norm_ffn_g, ffn_w_in, ffn_conv_w, ffn_conv_b, ffn_w_down, final_norm_g):
    batch, ctx_len, d = x_prompt.shape
    dec_batch, dec_seq, _ = x_sample.shape
    depth = ada_w.shape[0]
    assert depth == 1
    d_a = w_out_a.shape[1]
    d_b = w_fourier.shape[1]
    n_heads = d_a // HEAD
    d_ff = ffn_w_down.shape[1]
    ffp = -(-d_ff // 512) * 512
    lora_w = decay_up.shape[2]
    lora_a = iclr_up.shape[2]
    lora_g = gate_up.shape[1]
    group = d_b // N_GROUPS_B

    cvec = jnp.concatenate([c_ctx[None, :], c], axis=0).astype(F32)
    n_vec = cvec.shape[0]
    cvec = jnp.pad(cvec, ((0, 8 - n_vec), (0, 0)))
    mod = _modulation(cvec, ada_w[0].astype(F32), ada_b[0].astype(F32))[:n_vec].reshape(n_vec, 6, d)

    l = 0
    zpad_w = jnp.zeros((2, LANES - lora_w, d_a), F32)
    zpad_a = jnp.zeros((2, LANES - lora_a, d_a), F32)
    ffn_in = ffn_w_in[l]
    pad_ff = ((0, 0), (0, ffp - d_ff))
    cc, sc = _dft_mats(group)
    wts = {
        "d_a": d_a, "d_b": d_b, "lora_w": lora_w, "lora_a": lora_a, "lora_g": lora_g,
        "norm_mix_g": norm_mix_g[l], "w_in": w_in[l].astype(BF16),
        "rkv_conv_w": rkv_conv_w[l], "rkv_conv_b": rkv_conv_b[l].reshape(1, -1),
        "decay_up": jnp.concatenate([decay_up[l], zpad_w], axis=1),
        "iclr_up": jnp.concatenate([zpad_a, iclr_up[l]], axis=1),
        "decay_base": decay_base[l].reshape(2, 1, d_a), "iclr_base": iclr_base[l].reshape(2, 1, d_a),
        "k_k": k_k[l].reshape(1, d_a), "k_a": k_a[l].reshape(1, d_a), "r_k": r_k[l].reshape(1, d_a),
        "gate_up": gate_up[l].astype(BF16), "lnx_g": lnx_g[l], "lnx_b": lnx_b[l],
        "w_out_a": w_out_a[l].astype(BF16), "w_fourier": w_fourier[l].astype(BF16), "w_out": w_out[l].astype(BF16),
        "norm_ffn_g": norm_ffn_g[l],
        "ffn_w_in": jnp.concatenate([jnp.pad(ffn_in[:, :d_ff], pad_ff), jnp.pad(ffn_in[:, d_ff:], pad_ff)],
                                    axis=1).astype(BF16),
        "ffn_conv_w": jnp.concatenate([jnp.pad(ffn_conv_w[l][:, :d_ff], pad_ff),
                                       jnp.pad(ffn_conv_w[l][:, d_ff:], pad_ff)], axis=1),
        "ffn_conv_b": jnp.concatenate([jnp.pad(ffn_conv_b[l][None, :d_ff], pad_ff),
                                       jnp.pad(ffn_conv_b[l][None, d_ff:], pad_ff)], axis=1),
        "ffn_w_down": jnp.pad(ffn_w_down[l], ((0, ffp - d_ff), (0, 0))).astype(BF16),
        "final_norm_g": final_norm_g,
        "ones_bd": _blockdiag_ones(LANES), "avg_bd": _blockdiag_ones(256),
        "tri": (_tri_blockdiag(SCAN_TILE, False), _tri_blockdiag(SCAN_TILE, True)),
        "chan_cs": jnp.concatenate([cc, sc], axis=1).astype(BF16),
        "time_cs": {n: tuple(m.astype(BF16) for m in _dft_mats(n)) for n in {ctx_len, dec_seq}},
    }

    y_ctx, st_ctx = _layer(x_prompt.reshape(batch * ctx_len, d), mod[:1], None, batch, ctx_len, ctx_len,
                           wts, emit_state=True)
    s0 = _pair_layout(jnp.swapaxes(state_rwkv[:, l].astype(F32), -1, -2))
    y_lat, _ = _layer(x_sample.reshape(dec_batch * dec_seq, d), mod[1:], s0, dec_batch, dec_seq, GRID_W,
                      wts, emit_state=False)

    new_state = jnp.stack(st_ctx, axis=1)[:, None].astype(x_prompt.dtype)
    return (y_ctx.reshape(batch, ctx_len, d), y_lat.reshape(dec_batch, dec_seq, d), new_state)
```

```python
import functools
import math

import numpy as np
import jax
import jax.numpy as jnp
from jax import lax
from jax.experimental import pallas as pl
from jax.experimental.pallas import tpu as pltpu

F32 = jnp.float32
BF16 = jnp.bfloat16

HEAD = 64
LANES = 128
MXU_COLS = 256
FFN_SUB_ROWS = 256
CHUNK = 64
SCAN_TILE = 256
SCAN_PAIRS = 8
GRID_W = 64
N_GROUPS_B = 4
RMS_EPS = 1e-6
GN_EPS = 64e-5
V7X_VMEM_BYTES = 64 * 1024 * 1024
VMEM_LIMIT = V7X_VMEM_BYTES - 8 * 1024 * 1024
RESIDENT_DOUBLE_BUFFER_MAX = 4 * 1024 * 1024

NN = ((1,), (0,))
NT = ((1,), (1,))


def _cparams(sem):
    return pltpu.CompilerParams(dimension_semantics=sem, vmem_limit_bytes=VMEM_LIMIT)


def _dot(a, b, dims=NN):
    return lax.dot_general(a, b, (dims, ((), ())), preferred_element_type=F32)


def _split2(x):
    hi = x.astype(BF16)
    lo = (x - hi.astype(F32)).astype(BF16)
    return hi, lo


def _split3(x):
    hi = x.astype(BF16)
    r1 = x - hi.astype(F32)
    mid = r1.astype(BF16)
    lo = (r1 - mid.astype(F32)).astype(BF16)
    return hi, mid, lo


def _dot_split_rhs(a, b, dims=NN):
    ab = a.astype(BF16)
    bh, bl = _split2(b)
    return _dot(ab, bh, dims) + _dot(ab, bl, dims)


def _dot_exact_rhs(a, b_bf16, dims=NN):
    ah, al = _split2(a)
    return _dot(ah, b_bf16, dims) + _dot(al, b_bf16, dims)


def _sigmoid(x):
    return 1.0 / (1.0 + jnp.exp(-x))


def _mod_kernel(c_ref, w_ref, b_ref, o_ref):
    c = c_ref[...]
    s = c * _sigmoid(c)
    o_ref[...] = jnp.dot(s, w_ref[...], preferred_element_type=F32,
                         precision=lax.Precision.HIGHEST) + b_ref[...]


def _modulation(cvec, ada_w, ada_b):
    d, n = ada_w.shape
    tn = 1024 if n % 1024 == 0 else 512
    return pl.pallas_call(
        _mod_kernel,
        grid=(n // tn,),
        in_specs=[pl.BlockSpec((8, d), lambda j: (0, 0)),
                  pl.BlockSpec((d, tn), lambda j: (0, j)),
                  pl.BlockSpec((1, tn), lambda j: (0, j))],
        out_specs=pl.BlockSpec((8, tn), lambda j: (0, j)),
        out_shape=jax.ShapeDtypeStruct((8, n), F32),
        compiler_params=_cparams(("arbitrary",)),
        name="modulation",
    )(cvec, ada_w, ada_b.reshape(1, n))


def _rms_mod(x, g, shift, scale):
    ms = jnp.mean(x * x, axis=-1, keepdims=True)
    y = x * lax.rsqrt(ms + RMS_EPS) * g
    return y * (1.0 + scale) + shift


def _inproj_kernel(x_ref, mod_ref, g_ref, w_ref, o_ref, h_ref):
    @pl.when(pl.program_id(1) == 0)
    def _():
        h = _rms_mod(x_ref[...], g_ref[...], mod_ref[0:1, :], mod_ref[1:2, :])
        h_ref[...] = h.astype(BF16)

    o_ref[...] = jnp.dot(h_ref[...], w_ref[...], preferred_element_type=F32)


def _inproj(x, mod, g, w, rows_per_seg, tm, tn):
    r, d = x.shape
    n = w.shape[1]
    tps = rows_per_seg // tm
    return pl.pallas_call(
        _inproj_kernel,
        grid=(r // tm, n // tn),
        in_specs=[pl.BlockSpec((tm, d), lambda i, j: (i, 0)),
                  pl.BlockSpec((None, 6, d), lambda i, j: (i // tps, 0, 0)),
                  pl.BlockSpec((1, d), lambda i, j: (0, 0)),
                  pl.BlockSpec((d, tn), lambda i, j: (0, j))],
        out_specs=pl.BlockSpec((tm, tn), lambda i, j: (i, j)),
        out_shape=jax.ShapeDtypeStruct((r, n), F32),
        scratch_shapes=[pltpu.VMEM((tm, d), BF16)],
        compiler_params=_cparams(("parallel", "arbitrary")),
        name="inproj",
    )(x, mod, g.reshape(1, d), w)


def _conv3(x, w_ref, b_ref, pos, row_len):
    n = x.shape[0]
    prev = jnp.where(pos == 0, 0.0, pltpu.roll(x, 1, 0))
    nxt = jnp.where(pos == row_len - 1, 0.0, pltpu.roll(x, n - 1, 0))
    return prev * w_ref[0:1, :] + x * w_ref[1:2, :] + nxt * w_ref[2:3, :] + b_ref[...]


def _pair_diag(x, lane_lo):
    return jnp.concatenate([jnp.where(lane_lo, x, 0.0), jnp.where(lane_lo, 0.0, x)], axis=0)


def _scan_kernel(*refs, row_len, has_init, emit_state, n_pairs, shared_tile):
    it = iter(refs)
    take = lambda n: [next(it) for _ in range(n)]
    in_f = take(4)
    in_r = take(4)
    cwr_ref, cwk_ref, cwv_ref, cbr_ref, cbk_ref, cbv_ref = take(6)
    decup_refs = take(2)
    decb_refs = take(2)
    iclup_refs = take(2)
    iclb_refs = take(2)
    kkw_ref, kaw_ref, rkw_ref, ones_ref = take(4)
    tri_refs = take(2)
    s0_refs = take(2) if has_init else None
    y_refs = take(2)
    bonus_ref = next(it)
    sfin_refs = take(2) if emit_state else None
    s_ref = next(it)

    t = pl.program_id(2)
    n_t = pl.num_programs(2)
    tt, width = in_f[0].shape

    @pl.when(t == 0)
    def _():
        for dr in range(2):
            if has_init:
                s_ref[dr] = s0_refs[dr][...]
            else:
                s_ref[dr] = jnp.zeros(s_ref.shape[1:], F32)

    ones_bd = ones_ref[...]

    def b16(x):
        return x.astype(BF16)

    def head_sum(x):
        xb = b16(x)
        return jnp.concatenate([_dot(xb[:, p * LANES:(p + 1) * LANES], ones_bd) for p in range(n_pairs)], axis=1)

    pos = lax.broadcasted_iota(jnp.int32, (tt, width), 0) & (row_len - 1)
    row = lax.broadcasted_iota(jnp.int32, (CHUNK, LANES), 0)
    lane = lax.broadcasted_iota(jnp.int32, (CHUNK, LANES), 1)
    col = lane & (HEAD - 1)
    lane_lo = lane < HEAD
    same16 = (row >> 4) == (col >> 4)
    same32 = (row >> 5) == (col >> 5)
    off16 = jnp.logical_and(same32, jnp.logical_not(same16))
    eye = jnp.where(col == row, 1.0, 0.0).astype(F32)
    eye_t = (lax.broadcasted_iota(jnp.int32, (LANES, LANES), 0)
             == lax.broadcasted_iota(jnp.int32, (LANES, LANES), 1)).astype(BF16)

    def pd(x):
        return b16(_pair_diag(x, lane_lo))

    def mm(p, q_pd):
        return _dot(b16(p), q_pd)

    def tokens(refs_in, with_bonus):
        rp_ref, kp_ref, vp_ref, _ = refs_in
        r = _conv3(rp_ref[...], cwr_ref, cbr_ref, pos, row_len)
        k = _conv3(kp_ref[...], cwk_ref, cbk_ref, pos, row_len)
        v = _conv3(vp_ref[...], cwv_ref, cbv_ref, pos, row_len)
        kk = k * kkw_ref[...]
        kk = kk * lax.rsqrt(head_sum(kk * kk) + 1e-12)
        if with_bonus:
            bonus_ref[...] = head_sum(r * k * rkw_ref[...]) * v
        return dict(r=r, k=k, v=v, kk=kk)

    n_chunks = tt // CHUNK

    def direction(dr, tok_fn, dwda_ref):
        rev = dr == 1
        strict = (col > row) if rev else (col < row)
        incl = (col >= row) if rev else (col <= row)
        order = list(range(n_chunks - 1, -1, -1) if rev else range(n_chunks))
        last_row = 0 if rev else CHUNK - 1
        streams = [(c, p) for c in order for p in range(n_pairs)]
        w = {}
        q = {}
        state = {}

        def prep_tile():
            w.update(tok_fn())
            dwda = dwda_ref[...]
            w_logit = decb_refs[dr][...] + _dot_split_rhs(jnp.tanh(dwda), decup_refs[dr][...])
            w["logw"] = (-math.exp(-0.5)) * _sigmoid(w_logit)
            a = _sigmoid(iclb_refs[dr][...] + _dot_split_rhs(dwda, iclup_refs[dr][...]))
            w["kd"] = w["k"] * (1.0 + (a - 1.0) * kaw_ref[...])
            w["bb"] = w["kk"] * a
            l1, l2, l3 = _split3(w["logw"])
            tri = tri_refs[dr][...]
            w["cum"] = _dot(tri, l1) + (_dot(tri, l2) + _dot(tri, l3))
            for p in range(n_pairs):
                state[p] = s_ref[dr, p]

        def prep_chunk(c):
            def run():
                sl = slice(c * CHUNK, (c + 1) * CHUNK)
                cin = w["cum"][sl]
                c_end = cin[last_row:last_row + 1, :]
                g_inv = jnp.exp(-cin)
                g_rat = jnp.exp(c_end - cin)
                rt = w["r"][sl] * jnp.exp(cin)
                kkt = w["kk"][sl] * jnp.exp(cin - w["logw"][sl])
                kt = w["kd"][sl] * g_inv
                bt = w["bb"][sl] * g_inv
                kh = w["kd"][sl] * g_rat
                bh = w["bb"][sl] * g_rat
                vc = w["v"][sl]
                for p in range(n_pairs):
                    ls = slice(p * LANES, (p + 1) * LANES)
                    bk_t = b16(jnp.concatenate([bh[:, ls], kh[:, ls]], axis=0).T)
                    ce_t = jnp.broadcast_to(c_end[:, ls], (LANES, LANES)).T
                    g_col = jnp.exp(jnp.where(lane_lo, ce_t[:HEAD], ce_t[HEAD:]))
                    q[c, p] = dict(sl=sl, ls=ls, g_col=g_col, rt=b16(rt[:, ls]), kkt=kkt[:, ls], kt=kt[:, ls],
                                   bt=bt[:, ls], bk_t=bk_t, vc=vc[:, ls])
            return run

        def per_stream(fn):
            def run():
                for st in streams:
                    fn(q[st])
            return run

        def s_scores(d):
            x = jnp.concatenate([b16(d["kkt"]), d["rt"]], axis=0)
            sbk = _dot(x, jnp.concatenate([pd(d["bt"]), pd(d["kt"])], axis=0), NT)
            sb = sbk[:, :LANES]
            sk = sbk[:, LANES:]
            d["a_b"] = jnp.where(strict, sb[:CHUNK], 0.0)
            d["a_k"] = jnp.where(strict, sk[:CHUNK], 0.0)
            d["m_bk"] = b16(jnp.concatenate([jnp.where(incl, sb[CHUNK:], 0.0),
                                             jnp.where(incl, sk[CHUNK:], 0.0)], axis=1))
            d["n0"] = jnp.where(same16, d["a_b"], 0.0)
            d["vpd"] = pd(d["vc"])

        def s_n2(d):
            d["n2"] = mm(d["n0"], pd(d["n0"]))
            d["akv"] = mm(d["a_k"], d["vpd"])

        def s_n4(d):
            t1 = eye - d["n0"]
            both = _dot(b16(jnp.concatenate([d["n2"], t1], axis=0)), pd(d["n2"]))
            d["n4"] = both[:CHUNK]
            d["t"] = t1 + both[CHUNK:]

        def s_n8(d):
            both = _dot(b16(jnp.concatenate([d["n4"], d["t"]], axis=0)), pd(d["n4"]))
            d["n8"] = both[:CHUNK]
            d["t"] = d["t"] + both[CHUNK:]

        def s_t16(d):
            d["t"] = d["t"] + mm(d["t"], pd(d["n8"]))

        def s_p32(d):
            d["p"] = mm(d["t"], pd(jnp.where(off16, d["a_b"], 0.0)))

        def s_t(d):
            d["t"] = d["t"] - mm(d["p"], pd(d["t"]))

        def s_p64(d):
            d["p"] = mm(d["t"], pd(jnp.where(same32, 0.0, d["a_b"])))

        def s_w(d):
            w12 = _dot(b16(d["t"]), jnp.concatenate([pd(d["kkt"]), pd(d["akv"])], axis=1))
            d["w1r"] = jnp.concatenate([b16(w12[:, :LANES]), d["rt"]], axis=0)
            d["w2"] = w12[:, LANES:]

        def chain(c):
            def run():
                for p in range(n_pairs):
                    d = q[c, p]
                    xs = _dot(d["w1r"], pd(state[p]))
                    d["u"] = -(xs[:CHUNK] + d["w2"])
                    d["ys"] = xs[CHUNK:]
                    full = _dot(d["bk_t"], b16(jnp.concatenate([d["u"], d["vc"]], axis=0)))
                    state[p] = state[p] * d["g_col"] + jnp.where(lane_lo, full[:CHUNK], full[CHUNK:])
            return run

        def s_y(d):
            y = d["ys"] + _dot(d["m_bk"], jnp.concatenate([pd(d["u"]), d["vpd"]], axis=0))
            y_refs[dr][d["sl"], d["ls"]] = y.astype(y_refs[dr].dtype)

        def finish():
            for p in range(n_pairs):
                s_ref[dr, p] = state[p]
            if emit_state:
                @pl.when(t == n_t - 1)
                def _():
                    for p in range(n_pairs):
                        h1, h2, h3 = _split3(state[p])
                        tr = _dot(eye_t, h1, NT) + (_dot(eye_t, h2, NT) + _dot(eye_t, h3, NT))
                        sfin_refs[dr][2 * p] = tr[:HEAD]
                        sfin_refs[dr][2 * p + 1] = tr[HEAD:]

        prep = [prep_tile] + [prep_chunk(c) for c in order]
        work = [per_stream(f) for f in (s_scores, s_n2, s_n4, s_n8, s_t16, s_p32, s_t, s_p64, s_t, s_w)]
        work += [chain(c) for c in order] + [per_stream(s_y), finish]
        return prep, work

    tok_cache = {}

    def tok_f():
        tok_cache["f"] = tokens(in_f, True)
        return tok_cache["f"]

    def tok_r():
        return tok_cache["f"] if shared_tile else tokens(in_r, False)

    prep_f, work_f = direction(0, tok_f, in_f[3])
    prep_r, work_r = direction(1, tok_r, in_r[3])
    for fn in prep_f:
        fn()
    seq_a = work_f
    seq_b = prep_r + work_r
    for i in range(max(len(seq_a), len(seq_b))):
        if i < len(seq_a):
            seq_a[i]()
        if i < len(seq_b):
            seq_b[i]()


def _scan(proj, dwda_blk, conv_w, conv_b, dec_up, dec_base, icl_up, icl_base, k_k, k_a, r_k,
          ones_bd, tri, s0, *, batch, seq, d_a, row_len, emit_state):
    n_t = seq // SCAN_TILE
    n_p = d_a // LANES
    pps = SCAN_PAIRS
    assert n_p % pps == 0
    n_g = n_p // pps
    width = pps * LANES
    tt = SCAN_TILE
    rows = batch * seq

    def tile(rev):
        return (lambda b, t: b * n_t + (n_t - 1 - t)) if rev else (lambda b, t: b * n_t + t)

    def colspec(base, rev):
        ti = tile(rev)
        return pl.BlockSpec((tt, width), lambda b, g, t: (ti(b, t), base + g))

    def tokspecs(rev):
        ti = tile(rev)
        return [colspec(0, rev), colspec(n_g, rev), colspec(2 * n_g, rev),
                pl.BlockSpec((tt, LANES), lambda b, g, t: (ti(b, t), dwda_blk))]

    def wspec(nrow, base=0):
        return pl.BlockSpec((nrow, width), lambda b, g, t, base=base: (0, base + g))

    const2 = lambda shape: pl.BlockSpec(shape, lambda b, g, t: (0, 0))
    in_specs = (tokspecs(False) + tokspecs(True)
                + [wspec(3, 0), wspec(3, n_g), wspec(3, 2 * n_g), wspec(1, 0), wspec(1, n_g), wspec(1, 2 * n_g)]
                + [wspec(LANES)] * 2 + [wspec(1)] * 2 + [wspec(LANES)] * 2 + [wspec(1)] * 2
                + [wspec(1)] * 3 + [const2((LANES, LANES))] + [const2((tt, tt))] * 2)
    args = ([proj] * 8 + [conv_w] * 3 + [conv_b] * 3
            + [dec_up[0], dec_up[1], dec_base[0], dec_base[1], icl_up[0], icl_up[1], icl_base[0], icl_base[1]]
            + [k_k, k_a, r_k, ones_bd, tri[0], tri[1]])
    has_init = s0 is not None
    if has_init:
        for dr in range(2):
            in_specs.append(pl.BlockSpec((None, None, pps, HEAD, LANES), lambda b, g, t, dr=dr: (b, dr, g, 0, 0)))
            args.append(s0)
    yspec = lambda rev: colspec(0, rev)
    out_specs = [yspec(False), yspec(True), yspec(False)]
    out_shape = [jax.ShapeDtypeStruct((rows, d_a), BF16)] * 2 + [jax.ShapeDtypeStruct((rows, d_a), F32)]
    if emit_state:
        out_specs += [pl.BlockSpec((None, 2 * pps, HEAD, HEAD), lambda b, g, t: (b, g, 0, 0))] * 2
        out_shape += [jax.ShapeDtypeStruct((batch, 2 * n_p, HEAD, HEAD), F32)] * 2
    kern = functools.partial(_scan_kernel, row_len=row_len, has_init=has_init, emit_state=emit_state,
                             n_pairs=pps, shared_tile=(n_t == 1))
    return pl.pallas_call(
        kern,
        grid=(batch, n_g, n_t),
        in_specs=in_specs,
        out_specs=out_specs,
        out_shape=out_shape,
        scratch_shapes=[pltpu.VMEM((2, pps, HEAD, LANES), F32)],
        compiler_params=_cparams(("parallel", "parallel", "arbitrary")),
        name="rwkv7_scan_bidir",
    )(*args)


def _chan_dft_kernel(x_ref, w_ref, o_ref):
    group = w_ref.shape[0]
    w = w_ref[...]
    for g in range(x_ref.shape[1] // group):
        xg = x_ref[:, g * group:(g + 1) * group].astype(BF16)
        o_ref[:, 2 * g * group:2 * (g + 1) * group] = jnp.dot(xg, w, preferred_element_type=F32).astype(BF16)


def _chan_dft(proj, w_cs, xb_blk, d_b, tm):
    r = proj.shape[0]
    group = w_cs.shape[0]
    return pl.pallas_call(
        _chan_dft_kernel,
        grid=(r // tm,),
        in_specs=[pl.BlockSpec((tm, d_b), lambda i: (i, xb_blk)),
                  pl.BlockSpec((group, 2 * group), lambda i: (0, 0))],
        out_specs=pl.BlockSpec((tm, 2 * d_b), lambda i: (i, 0)),
        out_shape=jax.ShapeDtypeStruct((r, 2 * d_b), BF16),
        compiler_params=_cparams(("parallel",)),
        name="fourier_channels",
    )(proj, w_cs)


def _time_dft_kernel(ct_ref, st_ref, z_ref, o_ref):
    ct = ct_ref[...]
    st = st_ref[...]
    group = o_ref.shape[1] // N_GROUPS_B
    for g in range(N_GROUPS_B):
        zc = z_ref[:, 2 * g * group:(2 * g + 1) * group]
        zs = z_ref[:, (2 * g + 1) * group:(2 * g + 2) * group]
        y = jnp.dot(ct, zc, preferred_element_type=F32) - jnp.dot(st, zs, preferred_element_type=F32)
        o_ref[:, g * group:(g + 1) * group] = y.astype(BF16)


def _time_dft(z, ct, st, batch, seq, tm):
    n_m = seq // tm
    d_b = z.shape[1] // 2
    z_mode = pl.Buffered(1) if seq * 2 * d_b * z.dtype.itemsize > RESIDENT_DOUBLE_BUFFER_MAX else None
    return pl.pallas_call(
        _time_dft_kernel,
        grid=(batch, n_m),
        in_specs=[pl.BlockSpec((tm, seq), lambda b, m: (m, 0)),
                  pl.BlockSpec((tm, seq), lambda b, m: (m, 0)),
                  pl.BlockSpec((seq, 2 * d_b), lambda b, m: (b, 0), pipeline_mode=z_mode)],
        out_specs=pl.BlockSpec((tm, d_b), lambda b, m: (b * n_m + m, 0)),
        out_shape=jax.ShapeDtypeStruct((batch * seq, d_b), BF16),
        compiler_params=_cparams(("parallel", "arbitrary")),
        name="fourier_positions",
    )(ct, st, z)


def _dft_mats(n):
    scale = 1.0 / math.sqrt(n)
    m = math.isqrt(n)
    if m * m != n or n <= 1024:
        i = lax.broadcasted_iota(jnp.int32, (n, n), 0)
        j = lax.broadcasted_iota(jnp.int32, (n, n), 1)
        ang = ((i * j) % n).astype(F32) * (2.0 * math.pi / n)
        return jnp.cos(ang) * scale, jnp.sin(ang) * scale
    t = lax.broadcasted_iota(jnp.int32, (m, n), 0)
    f = lax.broadcasted_iota(jnp.int32, (m, n), 1)
    a1 = ((m * t * f) % n).astype(F32) * (2.0 * math.pi / n)
    a2 = ((t * f) % n).astype(F32) * (2.0 * math.pi / n)
    c1, s1 = (jnp.cos(a1) * scale)[:, None, :], (jnp.sin(a1) * scale)[:, None, :]
    c2, s2 = jnp.cos(a2)[None, :, :], jnp.sin(a2)[None, :, :]
    return (c1 * c2 - s1 * s2).reshape(n, n), (s1 * c2 + c1 * s2).reshape(n, n)


def _mix_out_kernel(yf_ref, yr_ref, bonus_ref, dg_ref, gup_ref, lg_ref, lb_ref, avg_ref, yb_ref,
                    ga0_ref, ga1_ref, gb0_ref, gb1_ref, wa_ref, wf_ref, wo_ref, x_ref, mod_ref, o_ref, ya_ref):
    avg = avg_ref[...]
    cw = avg.shape[0]
    sig = _sigmoid(dg_ref[...]).astype(BF16)
    for j in range(ya_ref.shape[1] // cw):
        cols = slice(j * cw, (j + 1) * cw)
        y = yf_ref[:, cols].astype(F32) + yr_ref[:, cols].astype(F32)
        mu = _dot_exact_rhs(y, avg) * (1.0 / HEAD)
        d = y - mu
        var = _dot_exact_rhs(d * d, avg) * (1.0 / HEAD)
        yn = d * lax.rsqrt(var + GN_EPS) * lg_ref[:, cols] + lb_ref[:, cols]
        g = jnp.dot(sig, gup_ref[:, cols], preferred_element_type=F32)
        ya_ref[:, cols] = ((yn + bonus_ref[:, cols]) * g).astype(BF16)

    half = ga0_ref.shape[1]
    ya = ya_ref[...]
    yb = yb_ref[...]
    parts = []
    for q, (ga_ref, gb_ref) in enumerate(((ga0_ref, gb0_ref), (ga1_ref, gb1_ref))):
        cols = slice(q * half, (q + 1) * half)
        pa = jnp.dot(ya, wa_ref[:, cols], preferred_element_type=F32)
        pb = jnp.dot(yb, wf_ref[:, cols], preferred_element_type=F32)
        parts.append((_sigmoid(ga_ref[...]) * pa + _sigmoid(gb_ref[...]) * pb).astype(BF16))
    mixed = jnp.concatenate(parts, axis=1)
    p = jnp.dot(mixed, wo_ref[...], preferred_element_type=F32)
    o_ref[...] = x_ref[...] + mod_ref[2:3, :] * p


def _mix_out(yf, yr, bonus, dg_blk, gate_up, lnx_g, lnx_b, avg, yb, proj, c_gates, w_a, w_f, w_o, x, mod,
             rows_per_seg, tm):
    r, d = x.shape
    d_a = yf.shape[1]
    d_b = yb.shape[1]
    lg = gate_up.shape[0]
    cw = avg.shape[0]
    half = d // 2
    assert c_gates % half == 0
    gblk = c_gates // half
    tps = rows_per_seg // tm
    resident = lambda shape: pl.BlockSpec(shape, lambda i: (0, 0), pipeline_mode=pl.Buffered(1))
    gate = lambda k: pl.BlockSpec((tm, half), lambda i, k=k: (i, gblk + k))
    tile_a = pl.BlockSpec((tm, d_a), lambda i: (i, 0))
    vec_a = pl.BlockSpec((1, d_a), lambda i: (0, 0))
    return pl.pallas_call(
        _mix_out_kernel,
        grid=(r // tm,),
        in_specs=[tile_a, tile_a, tile_a,
                  pl.BlockSpec((tm, lg), lambda i: (i, dg_blk)),
                  resident((lg, d_a)), vec_a, vec_a, resident((cw, cw)),
                  pl.BlockSpec((tm, d_b), lambda i: (i, 0)),
                  gate(0), gate(1), gate(2), gate(3),
                  resident((d_a, d)), resident((d_b, d)), resident((d, d)),
                  pl.BlockSpec((tm, d), lambda i: (i, 0)),
                  pl.BlockSpec((None, 6, d), lambda i: (i // tps, 0, 0))],
        out_specs=pl.BlockSpec((tm, d), lambda i: (i, 0)),
        out_shape=jax.ShapeDtypeStruct((r, d), F32),
        scratch_shapes=[pltpu.VMEM((tm, d_a), BF16)],
        compiler_params=_cparams(("parallel",)),
        name="headnorm_merge_outproj",
    )(yf, yr, bonus, proj, gate_up, lnx_g.reshape(1, d_a), lnx_b.reshape(1, d_a), avg, yb,
      proj, proj, proj, proj, w_a, w_f, w_o, x, mod)


def _ffn_in_kernel(x_ref, mod_ref, g_ref, wg_ref, wv_ref, cwg_ref, cwv_ref, cbg_ref, cbv_ref, o_ref, h_ref,
                   *, row_len):
    @pl.when(pl.program_id(1) == 0)
    def _():
        h = _rms_mod(x_ref[...], g_ref[...], mod_ref[3:4, :], mod_ref[4:5, :])
        h_ref[...] = h.astype(BF16)

    tm, tn = o_ref.shape
    sub_n = MXU_COLS
    sub_m = min(tm, FFN_SUB_ROWS)
    assert sub_m % row_len == 0
    pos = lax.broadcasted_iota(jnp.int32, (sub_m, sub_n), 0) & (row_len - 1)
    blocks = [(slice(m * sub_m, (m + 1) * sub_m), slice(q * sub_n, (q + 1) * sub_n))
              for m in range(tm // sub_m) for q in range(tn // sub_n)]

    def matmuls(blk):
        rows, cols = blk
        h = h_ref[rows, :]
        return (jnp.dot(h, wg_ref[:, cols], preferred_element_type=F32),
                jnp.dot(h, wv_ref[:, cols], preferred_element_type=F32))

    def epilogue(blk, ug, uv):
        rows, cols = blk
        ug = _conv3(ug, cwg_ref.at[:, cols], cbg_ref.at[:, cols], pos, row_len)
        uv = _conv3(uv, cwv_ref.at[:, cols], cbv_ref.at[:, cols], pos, row_len)
        o_ref[rows, cols] = (ug * _sigmoid(ug) * uv).astype(BF16)

    cur = matmuls(blocks[0])
    for i, blk in enumerate(blocks):
        nxt = matmuls(blocks[i + 1]) if i + 1 < len(blocks) else None
        epilogue(blk, *cur)
        cur = nxt


def _ffn_in(x, mod, g, w2, cw2, cb2, rows_per_seg, row_len, tm, tn):
    r, d = x.shape
    ffp = w2.shape[1] // 2
    nj = ffp // tn
    tps = rows_per_seg // tm
    return pl.pallas_call(
        functools.partial(_ffn_in_kernel, row_len=row_len),
        grid=(r // tm, nj),
        in_specs=[pl.BlockSpec((tm, d), lambda i, j: (i, 0)),
                  pl.BlockSpec((None, 6, d), lambda i, j: (i // tps, 0, 0)),
                  pl.BlockSpec((1, d), lambda i, j: (0, 0)),
                  pl.BlockSpec((d, tn), lambda i, j: (0, j)),
                  pl.BlockSpec((d, tn), lambda i, j: (0, nj + j)),
                  pl.BlockSpec((3, tn), lambda i, j: (0, j)),
                  pl.BlockSpec((3, tn), lambda i, j: (0, nj + j)),
                  pl.BlockSpec((1, tn), lambda i, j: (0, j)),
                  pl.BlockSpec((1, tn), lambda i, j: (0, nj + j))],
        out_specs=pl.BlockSpec((tm, tn), lambda i, j: (i, j)),
        out_shape=jax.ShapeDtypeStruct((r, ffp), BF16),
        scratch_shapes=[pltpu.VMEM((tm, d), BF16)],
        compiler_params=_cparams(("parallel", "arbitrary")),
        name="ffn_in_conv_gate",
    )(x, mod, g.reshape(1, d), w2, w2, cw2, cw2, cb2, cb2)


def _ffn_down_kernel(a_ref, w_ref, x_ref, mod_ref, g_ref, o_ref):
    x2 = x_ref[...] + mod_ref[5:6, :] * jnp.dot(a_ref[...], w_ref[...], preferred_element_type=F32)
    ms = jnp.mean(x2 * x2, axis=-1, keepdims=True)
    o_ref[...] = x2 * lax.rsqrt(ms + RMS_EPS) * g_ref[...]


def _ffn_down(act, w, x, mod, g, rows_per_seg, tm):
    r, d = x.shape
    ffp = act.shape[1]
    tps = rows_per_seg // tm
    return pl.pallas_call(
        _ffn_down_kernel,
        grid=(r // tm,),
        in_specs=[pl.BlockSpec((tm, ffp), lambda i: (i, 0)),
                  pl.BlockSpec((ffp, d), lambda i: (0, 0), pipeline_mode=pl.Buffered(1)),
                  pl.BlockSpec((tm, d), lambda i: (i, 0)),
                  pl.BlockSpec((None, 6, d), lambda i: (i // tps, 0, 0)),
                  pl.BlockSpec((1, d), lambda i: (0, 0))],
        out_specs=pl.BlockSpec((tm, d), lambda i: (i, 0)),
        out_shape=jax.ShapeDtypeStruct((r, d), F32),
        compiler_params=_cparams(("parallel",)),
        name="ffn_down_final_norm",
    )(act, w, x, mod, g.reshape(1, d))


def _pick(n, prefs):
    for p in prefs:
        if n % p == 0:
            return p
    raise ValueError(f"no tile in {prefs} divides {n}")


def _pair_layout(s):
    lead = s.shape[:-3]
    h = s.shape[-3]
    s = s.reshape(lead + (h // 2, 2, HEAD, HEAD))
    s = jnp.swapaxes(s, -3, -2)
    return s.reshape(lead + (h // 2, HEAD, 2 * HEAD))


def _layer(x, mod, s0, batch, seq, row_len, wts, emit_state):
    d = x.shape[1]
    rows = batch * seq
    rows_per_seg = rows // mod.shape[0]
    d_a = wts["d_a"]
    d_b = wts["d_b"]
    lora_w = wts["lora_w"]
    lora_g = wts["lora_g"]
    c_xb = 3 * d_a
    c_gates = c_xb + d_b
    c_dw = c_gates + 2 * d
    c_dg = c_dw + LANES
    assert lora_w == HEAD and wts["lora_a"] == HEAD and lora_g == LANES
    assert seq % SCAN_TILE == 0 and SCAN_TILE % row_len == 0 and row_len & (row_len - 1) == 0

    tm = _pick(rows_per_seg, (1024, 512, 256))
    n_in = wts["w_in"].shape[1]
    proj = _inproj(x, mod, wts["norm_mix_g"], wts["w_in"], rows_per_seg, tm, _pick(n_in, (1280, 768, 512, 256)))

    outs = _scan(proj, c_dw // LANES, wts["rkv_conv_w"], wts["rkv_conv_b"], wts["decay_up"], wts["decay_base"],
                 wts["iclr_up"], wts["iclr_base"], wts["k_k"], wts["k_a"], wts["r_k"], wts["ones_bd"], wts["tri"],
                 s0, batch=batch, seq=seq, d_a=d_a, row_len=row_len, emit_state=emit_state)
    ys, bonus, states = outs[:2], outs[2], list(outs[3:])

    assert c_xb % d_b == 0
    z = _chan_dft(proj, wts["chan_cs"], c_xb // d_b, d_b, _pick(rows, (512, 256)))
    ct, st = wts["time_cs"][seq]
    yb = _time_dft(z, ct, st, batch, seq, _pick(seq, (512, 256)))

    x1 = _mix_out(ys[0], ys[1], bonus, c_dg // lora_g, wts["gate_up"], wts["lnx_g"], wts["lnx_b"], wts["avg_bd"],
                  yb, proj, c_gates, wts["w_out_a"], wts["w_fourier"], wts["w_out"], x, mod, rows_per_seg, 256)

    act = _ffn_in(x1, mod, wts["norm_ffn_g"], wts["ffn_w_in"], wts["ffn_conv_w"], wts["ffn_conv_b"],
                  rows_per_seg, row_len, tm, 512)
    y = _ffn_down(act, wts["ffn_w_down"], x1, mod, wts["final_norm_g"], rows_per_seg, 256)
    return y, states


def _tri_blockdiag(n, rev):
    i = np.arange(n)[:, None]
    j = np.arange(n)[None, :]
    same = (i // CHUNK) == (j // CHUNK)
    m = same & ((j >= i) if rev else (j <= i))
    return jnp.asarray(m.astype(np.float32), dtype=BF16)


def _blockdiag_ones(n):
    i = np.arange(n)[:, None] // HEAD
    j = np.arange(n)[None, :] // HEAD
    return jnp.asarray((i == j).astype(np.float32), dtype=BF16)


def kernel(x_prompt, x_sample, state_rwkv, c, c_ctx, ada_w, ada_b, norm_mix_g, w_in, rkv_conv_w, rkv_conv_b,
           decay_up, decay_base, iclr_up, iclr_base, gate_up, k_k, k_a, r_k, lnx_g, lnx_b, w_out_a, w_fourier,
           w_out, norm_ffn_g, ffn_w_in, ffn_conv_w, ffn_conv_b, ffn_w_down, final_norm_g):
    batch, ctx_len, d = x_prompt.shape
    dec_batch, dec_seq, _ = x_sample.shape
    depth = ada_w.shape[0]
    assert depth == 1
    d_a = w_out_a.shape[1]
    d_b = w_fourier.shape[1]
    d_ff = ffn_w_down.shape[1]
    ffp = -(-d_ff // 512) * 512
    lora_w = decay_up.shape[2]
    lora_a = iclr_up.shape[2]
    lora_g = gate_up.shape[1]
    group = d_b // N_GROUPS_B

    cvec = jnp.concatenate([c_ctx[None, :], c], axis=0).astype(F32)
    n_vec = cvec.shape[0]
    cvec = jnp.pad(cvec, ((0, 8 - n_vec), (0, 0)))
    mod = _modulation(cvec, ada_w[0].astype(F32), ada_b[0].astype(F32))[:n_vec].reshape(n_vec, 6, d)

    l = 0
    zpad_w = jnp.zeros((2, LANES - lora_w, d_a), F32)
    zpad_a = jnp.zeros((2, LANES - lora_a, d_a), F32)
    ffn_in = ffn_w_in[l]
    pad_ff = ((0, 0), (0, ffp - d_ff))
    cc, sc = _dft_mats(group)
    wts = {
        "d_a": d_a, "d_b": d_b, "lora_w": lora_w, "lora_a": lora_a, "lora_g": lora_g,
        "norm_mix_g": norm_mix_g[l], "w_in": w_in[l].astype(BF16),
        "rkv_conv_w": rkv_conv_w[l], "rkv_conv_b": rkv_conv_b[l].reshape(1, -1),
        "decay_up": jnp.concatenate([decay_up[l], zpad_w], axis=1),
        "iclr_up": jnp.concatenate([zpad_a, iclr_up[l]], axis=1),
        "decay_base": decay_base[l].reshape(2, 1, d_a), "iclr_base": iclr_base[l].reshape(2, 1, d_a),
        "k_k": k_k[l].reshape(1, d_a), "k_a": k_a[l].reshape(1, d_a), "r_k": r_k[l].reshape(1, d_a),
        "gate_up": gate_up[l].astype(BF16), "lnx_g": lnx_g[l], "lnx_b": lnx_b[l],
        "w_out_a": w_out_a[l].astype(BF16), "w_fourier": w_fourier[l].astype(BF16), "w_out": w_out[l].astype(BF16),
        "norm_ffn_g": norm_ffn_g[l],
        "ffn_w_in": jnp.concatenate([jnp.pad(ffn_in[:, :d_ff].astype(BF16), pad_ff),
                                     jnp.pad(ffn_in[:, d_ff:].astype(BF16), pad_ff)], axis=1),
        "ffn_conv_w": jnp.concatenate([jnp.pad(ffn_conv_w[l][:, :d_ff], pad_ff),
                                       jnp.pad(ffn_conv_w[l][:, d_ff:], pad_ff)], axis=1),
        "ffn_conv_b": jnp.concatenate([jnp.pad(ffn_conv_b[l][None, :d_ff], pad_ff),
                                       jnp.pad(ffn_conv_b[l][None, d_ff:], pad_ff)], axis=1),
        "ffn_w_down": jnp.pad(ffn_w_down[l].astype(BF16), ((0, ffp - d_ff), (0, 0))),
        "final_norm_g": final_norm_g,
        "ones_bd": _blockdiag_ones(LANES), "avg_bd": _blockdiag_ones(256),
        "tri": (_tri_blockdiag(SCAN_TILE, False), _tri_blockdiag(SCAN_TILE, True)),
        "chan_cs": jnp.concatenate([cc, sc], axis=1).astype(BF16),
        "time_cs": {n: tuple(m.astype(BF16) for m in _dft_mats(n)) for n in {ctx_len, dec_seq}},
    }

    y_ctx, st_ctx = _layer(x_prompt.reshape(batch * ctx_len, d), mod[:1], None, batch, ctx_len, ctx_len,
                           wts, emit_state=True)
    s0 = _pair_layout(jnp.swapaxes(state_rwkv[:, l].astype(F32), -1, -2))
    y_lat, _ = _layer(x_sample.reshape(dec_batch * dec_seq, d), mod[1:], s0, dec_batch, dec_seq, GRID_W,
                      wts, emit_state=False)

    new_state = jnp.stack(st_ctx, axis=1)[:, None].astype(x_prompt.dtype)
    return (y_ctx.reshape(batch, ctx_len, d), y_lat.reshape(dec_batch, dec_seq, d), new_state)
```

```python
import functools
import math

import numpy as np
import jax
import jax.numpy as jnp
from jax import lax
from jax.experimental import pallas as pl
from jax.experimental.pallas import tpu as pltpu

F32 = jnp.float32
BF16 = jnp.bfloat16

HEAD = 64
LANES = 128
MXU_COLS = 256
FFN_SUB_ROWS = 256
CHUNK = 64
SCAN_TILE = 256
SCAN_PAIRS = 8
GRID_W = 64
N_GROUPS_B = 4
RMS_EPS = 1e-6
GN_EPS = 64e-5
V7X_VMEM_BYTES = 64 * 1024 * 1024
VMEM_LIMIT = V7X_VMEM_BYTES - 8 * 1024 * 1024
RESIDENT_DOUBLE_BUFFER_MAX = 4 * 1024 * 1024

NN = ((1,), (0,))
NT = ((1,), (1,))


def _cparams(sem):
    return pltpu.CompilerParams(dimension_semantics=sem, vmem_limit_bytes=VMEM_LIMIT)


def _dot(a, b, dims=NN):
    return lax.dot_general(a, b, (dims, ((), ())), preferred_element_type=F32)


def _split2(x):
    hi = x.astype(BF16)
    lo = (x - hi.astype(F32)).astype(BF16)
    return hi, lo


def _split3(x):
    hi = x.astype(BF16)
    r1 = x - hi.astype(F32)
    mid = r1.astype(BF16)
    lo = (r1 - mid.astype(F32)).astype(BF16)
    return hi, mid, lo


def _dot_split_rhs(a, b, dims=NN):
    ab = a.astype(BF16)
    bh, bl = _split2(b)
    return _dot(ab, bh, dims) + _dot(ab, bl, dims)


def _dot_exact_rhs(a, b_bf16, dims=NN):
    ah, al = _split2(a)
    return _dot(ah, b_bf16, dims) + _dot(al, b_bf16, dims)


def _sigmoid(x):
    return 1.0 / (1.0 + jnp.exp(-x))


def _pick(n, prefs):
    for p in prefs:
        if n % p == 0:
            return p
    raise ValueError(f"no tile in {prefs} divides {n}")


def _mod_kernel(c_ref, w_ref, b_ref, o_ref):
    c = c_ref[...]
    s = c * _sigmoid(c)
    o_ref[...] = jnp.dot(s, w_ref[...], preferred_element_type=F32,
                         precision=lax.Precision.HIGHEST) + b_ref[...]


def _modulation(cvec, ada_w, ada_b):
    d, n = ada_w.shape
    tn = _pick(n, (2048, 1024, 512))
    return pl.pallas_call(
        _mod_kernel,
        grid=(n // tn,),
        in_specs=[pl.BlockSpec((8, d), lambda j: (0, 0)),
                  pl.BlockSpec((d, tn), lambda j: (0, j)),
                  pl.BlockSpec((1, tn), lambda j: (0, j))],
        out_specs=pl.BlockSpec((8, tn), lambda j: (0, j)),
        out_shape=jax.ShapeDtypeStruct((8, n), F32),
        compiler_params=_cparams(("arbitrary",)),
        name="modulation",
    )(cvec, ada_w, ada_b.reshape(1, n))


def _rms_mod(x, g, shift, scale):
    ms = jnp.mean(x * x, axis=-1, keepdims=True)
    y = x * lax.rsqrt(ms + RMS_EPS) * g
    return y * (1.0 + scale) + shift


def _inproj_kernel(x_ref, mod_ref, g_ref, w_ref, o_ref, h_ref):
    @pl.when(pl.program_id(1) == 0)
    def _():
        h = _rms_mod(x_ref[...], g_ref[...], mod_ref[0:1, :], mod_ref[1:2, :])
        h_ref[...] = h.astype(BF16)

    o_ref[...] = jnp.dot(h_ref[...], w_ref[...], preferred_element_type=F32)


def _inproj(x, mod, g, w, rows_per_seg, tm, tn):
    r, d = x.shape
    n = w.shape[1]
    tps = rows_per_seg // tm
    return pl.pallas_call(
        _inproj_kernel,
        grid=(r // tm, n // tn),
        in_specs=[pl.BlockSpec((tm, d), lambda i, j: (i, 0)),
                  pl.BlockSpec((None, 6, d), lambda i, j: (i // tps, 0, 0)),
                  pl.BlockSpec((1, d), lambda i, j: (0, 0)),
                  pl.BlockSpec((d, tn), lambda i, j: (0, j))],
        out_specs=pl.BlockSpec((tm, tn), lambda i, j: (i, j)),
        out_shape=jax.ShapeDtypeStruct((r, n), F32),
        scratch_shapes=[pltpu.VMEM((tm, d), BF16)],
        compiler_params=_cparams(("parallel", "arbitrary")),
        name="inproj",
    )(x, mod, g.reshape(1, d), w)


def _conv3(x, w_ref, b_ref, pos, row_len):
    n = x.shape[0]
    prev = jnp.where(pos == 0, 0.0, pltpu.roll(x, 1, 0))
    nxt = jnp.where(pos == row_len - 1, 0.0, pltpu.roll(x, n - 1, 0))
    return prev * w_ref[0:1, :] + x * w_ref[1:2, :] + nxt * w_ref[2:3, :] + b_ref[...]


def _pair_diag(x, lane_lo):
    return jnp.concatenate([jnp.where(lane_lo, x, 0.0), jnp.where(lane_lo, 0.0, x)], axis=0)


def _scan_kernel(*refs, row_len, has_init, emit_state, n_pairs, shared_tile):
    it = iter(refs)
    take = lambda n: [next(it) for _ in range(n)]
    in_f = take(4)
    in_r = take(4)
    cwr_ref, cwk_ref, cwv_ref, cbr_ref, cbk_ref, cbv_ref = take(6)
    decup_refs = take(2)
    decb_refs = take(2)
    iclup_refs = take(2)
    iclb_refs = take(2)
    kkw_ref, kaw_ref, rkw_ref, ones_ref = take(4)
    tri_refs = take(2)
    s0_refs = take(2) if has_init else None
    y_refs = take(2)
    bonus_ref = next(it)
    sfin_refs = take(2) if emit_state else None
    s_ref = next(it)

    t = pl.program_id(2)
    n_t = pl.num_programs(2)
    tt, width = in_f[0].shape

    @pl.when(t == 0)
    def _():
        for dr in range(2):
            if has_init:
                s_ref[dr] = s0_refs[dr][...]
            else:
                s_ref[dr] = jnp.zeros(s_ref.shape[1:], F32)

    ones_bd = ones_ref[...]

    def b16(x):
        return x.astype(BF16)

    def head_sum(x):
        xb = b16(x)
        return jnp.concatenate([_dot(xb[:, p * LANES:(p + 1) * LANES], ones_bd) for p in range(n_pairs)], axis=1)

    pos = lax.broadcasted_iota(jnp.int32, (tt, width), 0) & (row_len - 1)
    row = lax.broadcasted_iota(jnp.int32, (CHUNK, LANES), 0)
    lane = lax.broadcasted_iota(jnp.int32, (CHUNK, LANES), 1)
    col = lane & (HEAD - 1)
    lane_lo = lane < HEAD
    same16 = (row >> 4) == (col >> 4)
    same32 = (row >> 5) == (col >> 5)
    off16 = jnp.logical_and(same32, jnp.logical_not(same16))
    eye = jnp.where(col == row, 1.0, 0.0).astype(F32)
    eye_t = (lax.broadcasted_iota(jnp.int32, (LANES, LANES), 0)
             == lax.broadcasted_iota(jnp.int32, (LANES, LANES), 1)).astype(BF16)

    def pd(x):
        return b16(_pair_diag(x, lane_lo))

    def mm(p, q_pd):
        return _dot(b16(p), q_pd)

    def tokens(refs_in, with_bonus):
        rp_ref, kp_ref, vp_ref, _ = refs_in
        r = _conv3(rp_ref[...], cwr_ref, cbr_ref, pos, row_len)
        k = _conv3(kp_ref[...], cwk_ref, cbk_ref, pos, row_len)
        v = _conv3(vp_ref[...], cwv_ref, cbv_ref, pos, row_len)
        kk = k * kkw_ref[...]
        kk = kk * lax.rsqrt(head_sum(kk * kk) + 1e-12)
        if with_bonus:
            bonus_ref[...] = head_sum(r * k * rkw_ref[...]) * v
        return dict(r=r, k=k, v=v, kk=kk)

    n_chunks = tt // CHUNK

    def direction(dr, tok_fn, dwda_ref):
        rev = dr == 1
        strict = (col > row) if rev else (col < row)
        incl = (col >= row) if rev else (col <= row)
        order = list(range(n_chunks - 1, -1, -1) if rev else range(n_chunks))
        last_row = 0 if rev else CHUNK - 1
        streams = [(c, p) for c in order for p in range(n_pairs)]
        w = {}
        q = {}
        state = {}

        def prep_tile():
            w.update(tok_fn())
            dwda = dwda_ref[...]
            w_logit = decb_refs[dr][...] + _dot_split_rhs(jnp.tanh(dwda), decup_refs[dr][...])
            w["logw"] = (-math.exp(-0.5)) * _sigmoid(w_logit)
            a = _sigmoid(iclb_refs[dr][...] + _dot_split_rhs(dwda, iclup_refs[dr][...]))
            w["kd"] = w["k"] * (1.0 + (a - 1.0) * kaw_ref[...])
            w["bb"] = w["kk"] * a
            l1, l2, l3 = _split3(w["logw"])
            tri = tri_refs[dr][...]
            w["cum"] = _dot(tri, l1) + (_dot(tri, l2) + _dot(tri, l3))
            for p in range(n_pairs):
                state[p] = s_ref[dr, p]

        def prep_chunk(c):
            def run():
                sl = slice(c * CHUNK, (c + 1) * CHUNK)
                cin = w["cum"][sl]
                c_end = cin[last_row:last_row + 1, :]
                g_inv = jnp.exp(-cin)
                g_rat = jnp.exp(c_end - cin)
                rt = w["r"][sl] * jnp.exp(cin)
                kkt = w["kk"][sl] * jnp.exp(cin - w["logw"][sl])
                kt = w["kd"][sl] * g_inv
                bt = w["bb"][sl] * g_inv
                kh = w["kd"][sl] * g_rat
                bh = w["bb"][sl] * g_rat
                vc = w["v"][sl]
                for p in range(n_pairs):
                    ls = slice(p * LANES, (p + 1) * LANES)
                    bk_t = b16(jnp.concatenate([bh[:, ls], kh[:, ls]], axis=0).T)
                    ce_t = jnp.broadcast_to(c_end[:, ls], (LANES, LANES)).T
                    g_col = jnp.exp(jnp.where(lane_lo, ce_t[:HEAD], ce_t[HEAD:]))
                    q[c, p] = dict(sl=sl, ls=ls, g_col=g_col, rt=b16(rt[:, ls]), kkt=kkt[:, ls], kt=kt[:, ls],
                                   bt=bt[:, ls], bk_t=bk_t, vc=vc[:, ls])
            return run

        def per_stream(fn):
            def run():
                for st in streams:
                    fn(q[st])
            return run

        def s_scores(d):
            x = jnp.concatenate([b16(d["kkt"]), d["rt"]], axis=0)
            sbk = _dot(x, jnp.concatenate([pd(d["bt"]), pd(d["kt"])], axis=0), NT)
            sb = sbk[:, :LANES]
            sk = sbk[:, LANES:]
            d["a_b"] = jnp.where(strict, sb[:CHUNK], 0.0)
            d["a_k"] = jnp.where(strict, sk[:CHUNK], 0.0)
            d["m_bk"] = b16(jnp.concatenate([jnp.where(incl, sb[CHUNK:], 0.0),
                                             jnp.where(incl, sk[CHUNK:], 0.0)], axis=1))
            d["n0"] = jnp.where(same16, d["a_b"], 0.0)
            d["vpd"] = pd(d["vc"])

        def s_n2(d):
            d["n2"] = mm(d["n0"], pd(d["n0"]))
            d["akv"] = mm(d["a_k"], d["vpd"])

        def s_n4(d):
            t1 = eye - d["n0"]
            both = _dot(b16(jnp.concatenate([d["n2"], t1], axis=0)), pd(d["n2"]))
            d["n4"] = both[:CHUNK]
            d["t"] = t1 + both[CHUNK:]

        def s_n8(d):
            both = _dot(b16(jnp.concatenate([d["n4"], d["t"]], axis=0)), pd(d["n4"]))
            d["n8"] = both[:CHUNK]
            d["t"] = d["t"] + both[CHUNK:]

        def s_t16(d):
            d["t"] = d["t"] + mm(d["t"], pd(d["n8"]))

        def s_p32(d):
            d["p"] = mm(d["t"], pd(jnp.where(off16, d["a_b"], 0.0)))

        def s_t(d):
            d["t"] = d["t"] - mm(d["p"], pd(d["t"]))

        def s_p64(d):
            d["p"] = mm(d["t"], pd(jnp.where(same32, 0.0, d["a_b"])))

        def s_w(d):
            w12 = _dot(b16(d["t"]), jnp.concatenate([pd(d["kkt"]), pd(d["akv"])], axis=1))
            d["w1r"] = jnp.concatenate([b16(w12[:, :LANES]), d["rt"]], axis=0)
            d["w2"] = w12[:, LANES:]

        def chain(c):
            def run():
                for p in range(n_pairs):
                    d = q[c, p]
                    xs = _dot(d["w1r"], pd(state[p]))
                    d["u"] = -(xs[:CHUNK] + d["w2"])
                    d["ys"] = xs[CHUNK:]
                    full = _dot(d["bk_t"], b16(jnp.concatenate([d["u"], d["vc"]], axis=0)))
                    state[p] = state[p] * d["g_col"] + jnp.where(lane_lo, full[:CHUNK], full[CHUNK:])
            return run

        def s_y(d):
            y = d["ys"] + _dot(d["m_bk"], jnp.concatenate([pd(d["u"]), d["vpd"]], axis=0))
            y_refs[dr][d["sl"], d["ls"]] = y.astype(y_refs[dr].dtype)

        def finish():
            for p in range(n_pairs):
                s_ref[dr, p] = state[p]
            if emit_state:
                @pl.when(t == n_t - 1)
                def _():
                    for p in range(n_pairs):
                        h1, h2, h3 = _split3(state[p])
                        tr = _dot(eye_t, h1, NT) + (_dot(eye_t, h2, NT) + _dot(eye_t, h3, NT))
                        sfin_refs[dr][2 * p] = tr[:HEAD]
                        sfin_refs[dr][2 * p + 1] = tr[HEAD:]

        prep = [prep_tile] + [prep_chunk(c) for c in order]
        work = [per_stream(f) for f in (s_scores, s_n2, s_n4, s_n8, s_t16, s_p32, s_t, s_p64, s_t, s_w)]
        work += [chain(c) for c in order] + [per_stream(s_y), finish]
        return prep, work

    tok_cache = {}

    def tok_f():
        tok_cache["f"] = tokens(in_f, True)
        return tok_cache["f"]

    def tok_r():
        return tok_cache["f"] if shared_tile else tokens(in_r, False)

    prep_f, work_f = direction(0, tok_f, in_f[3])
    prep_r, work_r = direction(1, tok_r, in_r[3])
    for fn in prep_f:
        fn()
    seq_a = work_f
    seq_b = prep_r + work_r
    for i in range(max(len(seq_a), len(seq_b))):
        if i < len(seq_a):
            seq_a[i]()
        if i < len(seq_b):
            seq_b[i]()


def _scan(proj, dwda_blk, conv_w, conv_b, dec_up, dec_base, icl_up, icl_base, k_k, k_a, r_k,
          ones_bd, tri, s0, *, batch, seq, d_a, row_len, emit_state):
    n_t = seq // SCAN_TILE
    n_p = d_a // LANES
    pps = SCAN_PAIRS
    assert n_p % pps == 0
    n_g = n_p // pps
    width = pps * LANES
    tt = SCAN_TILE
    rows = batch * seq

    def tile(rev):
        return (lambda b, t: b * n_t + (n_t - 1 - t)) if rev else (lambda b, t: b * n_t + t)

    def colspec(base, rev):
        ti = tile(rev)
        return pl.BlockSpec((tt, width), lambda b, g, t: (ti(b, t), base + g))

    def tokspecs(rev):
        ti = tile(rev)
        return [colspec(0, rev), colspec(n_g, rev), colspec(2 * n_g, rev),
                pl.BlockSpec((tt, LANES), lambda b, g, t: (ti(b, t), dwda_blk))]

    def wspec(nrow, base=0):
        return pl.BlockSpec((nrow, width), lambda b, g, t, base=base: (0, base + g))

    const2 = lambda shape: pl.BlockSpec(shape, lambda b, g, t: (0, 0))
    in_specs = (tokspecs(False) + tokspecs(True)
                + [wspec(3, 0), wspec(3, n_g), wspec(3, 2 * n_g), wspec(1, 0), wspec(1, n_g), wspec(1, 2 * n_g)]
                + [wspec(LANES)] * 2 + [wspec(1)] * 2 + [wspec(LANES)] * 2 + [wspec(1)] * 2
                + [wspec(1)] * 3 + [const2((LANES, LANES))] + [const2((tt, tt))] * 2)
    args = ([proj] * 8 + [conv_w] * 3 + [conv_b] * 3
            + [dec_up[0], dec_up[1], dec_base[0], dec_base[1], icl_up[0], icl_up[1], icl_base[0], icl_base[1]]
            + [k_k, k_a, r_k, ones_bd, tri[0], tri[1]])
    has_init = s0 is not None
    if has_init:
        for dr in range(2):
            in_specs.append(pl.BlockSpec((None, None, pps, HEAD, LANES), lambda b, g, t, dr=dr: (b, dr, g, 0, 0)))
            args.append(s0)
    yspec = lambda rev: colspec(0, rev)
    out_specs = [yspec(False), yspec(True), yspec(False)]
    out_shape = [jax.ShapeDtypeStruct((rows, d_a), BF16)] * 2 + [jax.ShapeDtypeStruct((rows, d_a), F32)]
    if emit_state:
        out_specs += [pl.BlockSpec((None, 2 * pps, HEAD, HEAD), lambda b, g, t: (b, g, 0, 0))] * 2
        out_shape += [jax.ShapeDtypeStruct((batch, 2 * n_p, HEAD, HEAD), F32)] * 2
    kern = functools.partial(_scan_kernel, row_len=row_len, has_init=has_init, emit_state=emit_state,
                             n_pairs=pps, shared_tile=(n_t == 1))
    return pl.pallas_call(
        kern,
        grid=(batch, n_g, n_t),
        in_specs=in_specs,
        out_specs=out_specs,
        out_shape=out_shape,
        scratch_shapes=[pltpu.VMEM((2, pps, HEAD, LANES), F32)],
        compiler_params=_cparams(("parallel", "parallel", "arbitrary")),
        name="rwkv7_scan_bidir",
    )(*args)


def _chan_dft_kernel(x_ref, w_ref, o_ref):
    group = w_ref.shape[0]
    w = w_ref[...]
    for g in range(x_ref.shape[1] // group):
        xg = x_ref[:, g * group:(g + 1) * group].astype(BF16)
        o_ref[:, 2 * g * group:2 * (g + 1) * group] = jnp.dot(xg, w, preferred_element_type=F32).astype(BF16)


def _chan_dft(proj, w_cs, xb_blk, d_b, tm):
    r = proj.shape[0]
    group = w_cs.shape[0]
    return pl.pallas_call(
        _chan_dft_kernel,
        grid=(r // tm,),
        in_specs=[pl.BlockSpec((tm, d_b), lambda i: (i, xb_blk)),
                  pl.BlockSpec((group, 2 * group), lambda i: (0, 0))],
        out_specs=pl.BlockSpec((tm, 2 * d_b), lambda i: (i, 0)),
        out_shape=jax.ShapeDtypeStruct((r, 2 * d_b), BF16),
        compiler_params=_cparams(("parallel",)),
        name="fourier_channels",
    )(proj, w_cs)


def _time_dft_kernel(ct_ref, st_ref, z_ref, o_ref):
    ct = ct_ref[...]
    st = st_ref[...]
    group = o_ref.shape[1] // N_GROUPS_B
    for g in range(N_GROUPS_B):
        zc = z_ref[:, 2 * g * group:(2 * g + 1) * group]
        zs = z_ref[:, (2 * g + 1) * group:(2 * g + 2) * group]
        y = jnp.dot(ct, zc, preferred_element_type=F32) - jnp.dot(st, zs, preferred_element_type=F32)
        o_ref[:, g * group:(g + 1) * group] = y.astype(BF16)


def _time_dft(z, ct, st, batch, seq, tm):
    n_m = seq // tm
    d_b = z.shape[1] // 2
    z_mode = pl.Buffered(1) if seq * 2 * d_b * z.dtype.itemsize > RESIDENT_DOUBLE_BUFFER_MAX else None
    return pl.pallas_call(
        _time_dft_kernel,
        grid=(batch, n_m),
        in_specs=[pl.BlockSpec((tm, seq), lambda b, m: (m, 0)),
                  pl.BlockSpec((tm, seq), lambda b, m: (m, 0)),
                  pl.BlockSpec((seq, 2 * d_b), lambda b, m: (b, 0), pipeline_mode=z_mode)],
        out_specs=pl.BlockSpec((tm, d_b), lambda b, m: (b * n_m + m, 0)),
        out_shape=jax.ShapeDtypeStruct((batch * seq, d_b), BF16),
        compiler_params=_cparams(("parallel", "arbitrary")),
        name="fourier_positions",
    )(ct, st, z)


def _dft_mats(n):
    scale = 1.0 / math.sqrt(n)
    m = math.isqrt(n)
    if m * m != n or n <= 1024:
        i = lax.broadcasted_iota(jnp.int32, (n, n), 0)
        j = lax.broadcasted_iota(jnp.int32, (n, n), 1)
        ang = ((i * j) % n).astype(F32) * (2.0 * math.pi / n)
        return jnp.cos(ang) * scale, jnp.sin(ang) * scale
    t = lax.broadcasted_iota(jnp.int32, (m, n), 0)
    f = lax.broadcasted_iota(jnp.int32, (m, n), 1)
    a1 = ((m * t * f) % n).astype(F32) * (2.0 * math.pi / n)
    a2 = ((t * f) % n).astype(F32) * (2.0 * math.pi / n)
    c1, s1 = (jnp.cos(a1) * scale)[:, None, :], (jnp.sin(a1) * scale)[:, None, :]
    c2, s2 = jnp.cos(a2)[None, :, :], jnp.sin(a2)[None, :, :]
    return (c1 * c2 - s1 * s2).reshape(n, n), (s1 * c2 + c1 * s2).reshape(n, n)


def _mix_out_kernel(yf_ref, yr_ref, bonus_ref, dg_ref, gup_ref, lg_ref, lb_ref, avg_ref, yb_ref,
                    ga0_ref, ga1_ref, gb0_ref, gb1_ref, wa_ref, wf_ref, wo_ref, x_ref, mod_ref, o_ref, ya_ref):
    avg = avg_ref[...]
    cw = avg.shape[0]
    sig = _sigmoid(dg_ref[...]).astype(BF16)
    for j in range(ya_ref.shape[1] // cw):
        cols = slice(j * cw, (j + 1) * cw)
        y = yf_ref[:, cols].astype(F32) + yr_ref[:, cols].astype(F32)
        mu = _dot_exact_rhs(y, avg) * (1.0 / HEAD)
        d = y - mu
        var = _dot_exact_rhs(d * d, avg) * (1.0 / HEAD)
        yn = d * lax.rsqrt(var + GN_EPS) * lg_ref[:, cols] + lb_ref[:, cols]
        g = jnp.dot(sig, gup_ref[:, cols], preferred_element_type=F32)
        ya_ref[:, cols] = ((yn + bonus_ref[:, cols]) * g).astype(BF16)

    half = ga0_ref.shape[1]
    ya = ya_ref[...]
    yb = yb_ref[...]
    parts = []
    for q, (ga_ref, gb_ref) in enumerate(((ga0_ref, gb0_ref), (ga1_ref, gb1_ref))):
        cols = slice(q * half, (q + 1) * half)
        pa = jnp.dot(ya, wa_ref[:, cols], preferred_element_type=F32)
        pb = jnp.dot(yb, wf_ref[:, cols], preferred_element_type=F32)
        parts.append((_sigmoid(ga_ref[...]) * pa + _sigmoid(gb_ref[...]) * pb).astype(BF16))
    mixed = jnp.concatenate(parts, axis=1)
    p = jnp.dot(mixed, wo_ref[...], preferred_element_type=F32)
    o_ref[...] = x_ref[...] + mod_ref[2:3, :] * p


def _mix_out(yf, yr, bonus, dg_blk, gate_up, lnx_g, lnx_b, avg, yb, proj, c_gates, w_a, w_f, w_o, x, mod,
             rows_per_seg, tm):
    r, d = x.shape
    d_a = yf.shape[1]
    d_b = yb.shape[1]
    lg = gate_up.shape[0]
    cw = avg.shape[0]
    half = d // 2
    assert c_gates % half == 0
    gblk = c_gates // half
    tps = rows_per_seg // tm
    resident = lambda shape: pl.BlockSpec(shape, lambda i: (0, 0), pipeline_mode=pl.Buffered(1))
    gate = lambda k: pl.BlockSpec((tm, half), lambda i, k=k: (i, gblk + k))
    tile_a = pl.BlockSpec((tm, d_a), lambda i: (i, 0))
    vec_a = pl.BlockSpec((1, d_a), lambda i: (0, 0))
    return pl.pallas_call(
        _mix_out_kernel,
        grid=(r // tm,),
        in_specs=[tile_a, tile_a, tile_a,
                  pl.BlockSpec((tm, lg), lambda i: (i, dg_blk)),
                  resident((lg, d_a)), vec_a, vec_a, resident((cw, cw)),
                  pl.BlockSpec((tm, d_b), lambda i: (i, 0)),
                  gate(0), gate(1), gate(2), gate(3),
                  resident((d_a, d)), resident((d_b, d)), resident((d, d)),
                  pl.BlockSpec((tm, d), lambda i: (i, 0)),
                  pl.BlockSpec((None, 6, d), lambda i: (i // tps, 0, 0))],
        out_specs=pl.BlockSpec((tm, d), lambda i: (i, 0)),
        out_shape=jax.ShapeDtypeStruct((r, d), F32),
        scratch_shapes=[pltpu.VMEM((tm, d_a), BF16)],
        compiler_params=_cparams(("parallel",)),
        name="headnorm_merge_outproj",
    )(yf, yr, bonus, proj, gate_up, lnx_g.reshape(1, d_a), lnx_b.reshape(1, d_a), avg, yb,
      proj, proj, proj, proj, w_a, w_f, w_o, x, mod)


def _ffn_in_kernel(x_ref, mod_ref, g_ref, wg_ref, wv_ref, cwg_ref, cwv_ref, cbg_ref, cbv_ref, o_ref, h_ref,
                   *, row_len):
    @pl.when(pl.program_id(1) == 0)
    def _():
        h = _rms_mod(x_ref[...], g_ref[...], mod_ref[3:4, :], mod_ref[4:5, :])
        h_ref[...] = h.astype(BF16)

    tm, tn = o_ref.shape
    sub_n = MXU_COLS
    sub_m = min(tm, FFN_SUB_ROWS)
    assert sub_m % row_len == 0
    pos = lax.broadcasted_iota(jnp.int32, (sub_m, sub_n), 0) & (row_len - 1)
    blocks = [(slice(m * sub_m, (m + 1) * sub_m), slice(q * sub_n, (q + 1) * sub_n))
              for m in range(tm // sub_m) for q in range(tn // sub_n)]

    def matmuls(blk):
        rows, cols = blk
        h = h_ref[rows, :]
        return (jnp.dot(h, wg_ref[:, cols], preferred_element_type=F32),
                jnp.dot(h, wv_ref[:, cols], preferred_element_type=F32))

    def epilogue(blk, ug, uv):
        rows, cols = blk
        ug = _conv3(ug, cwg_ref.at[:, cols], cbg_ref.at[:, cols], pos, row_len)
        uv = _conv3(uv, cwv_ref.at[:, cols], cbv_ref.at[:, cols], pos, row_len)
        o_ref[rows, cols] = (ug * _sigmoid(ug) * uv).astype(BF16)

    cur = matmuls(blocks[0])
    for i, blk in enumerate(blocks):
        nxt = matmuls(blocks[i + 1]) if i + 1 < len(blocks) else None
        epilogue(blk, *cur)
        cur = nxt


def _ffn_in(x, mod, g, w_gv, cw_gv, cb_gv, rows_per_seg, row_len, tm, tn):
    r, d = x.shape
    ffp = w_gv[0].shape[1]
    tps = rows_per_seg // tm
    wspec = pl.BlockSpec((d, tn), lambda i, j: (0, j))
    cwspec = pl.BlockSpec((3, tn), lambda i, j: (0, j))
    cbspec = pl.BlockSpec((1, tn), lambda i, j: (0, j))
    return pl.pallas_call(
        functools.partial(_ffn_in_kernel, row_len=row_len),
        grid=(r // tm, ffp // tn),
        in_specs=[pl.BlockSpec((tm, d), lambda i, j: (i, 0)),
                  pl.BlockSpec((None, 6, d), lambda i, j: (i // tps, 0, 0)),
                  pl.BlockSpec((1, d), lambda i, j: (0, 0)),
                  wspec, wspec, cwspec, cwspec, cbspec, cbspec],
        out_specs=pl.BlockSpec((tm, tn), lambda i, j: (i, j)),
        out_shape=jax.ShapeDtypeStruct((r, ffp), BF16),
        scratch_shapes=[pltpu.VMEM((tm, d), BF16)],
        compiler_params=_cparams(("parallel", "arbitrary")),
        name="ffn_in_conv_gate",
    )(x, mod, g.reshape(1, d), *w_gv, *cw_gv, *cb_gv)


def _ffn_down_kernel(a_ref, w_ref, x_ref, mod_ref, g_ref, o_ref):
    x2 = x_ref[...] + mod_ref[5:6, :] * jnp.dot(a_ref[...], w_ref[...], preferred_element_type=F32)
    ms = jnp.mean(x2 * x2, axis=-1, keepdims=True)
    o_ref[...] = x2 * lax.rsqrt(ms + RMS_EPS) * g_ref[...]


def _ffn_down(act, w, x, mod, g, rows_per_seg, tm):
    r, d = x.shape
    ffp = act.shape[1]
    tps = rows_per_seg // tm
    return pl.pallas_call(
        _ffn_down_kernel,
        grid=(r // tm,),
        in_specs=[pl.BlockSpec((tm, ffp), lambda i: (i, 0)),
                  pl.BlockSpec((ffp, d), lambda i: (0, 0), pipeline_mode=pl.Buffered(1)),
                  pl.BlockSpec((tm, d), lambda i: (i, 0)),
                  pl.BlockSpec((None, 6, d), lambda i: (i // tps, 0, 0)),
                  pl.BlockSpec((1, d), lambda i: (0, 0))],
        out_specs=pl.BlockSpec((tm, d), lambda i: (i, 0)),
        out_shape=jax.ShapeDtypeStruct((r, d), F32),
        compiler_params=_cparams(("parallel",)),
        name="ffn_down_final_norm",
    )(act, w, x, mod, g.reshape(1, d))


def _pair_layout(s):
    lead = s.shape[:-3]
    h = s.shape[-3]
    s = s.reshape(lead + (h // 2, 2, HEAD, HEAD))
    s = jnp.swapaxes(s, -3, -2)
    return s.reshape(lead + (h // 2, HEAD, 2 * HEAD))


def _layer(x, mod, s0, batch, seq, row_len, wts, emit_state):
    d = x.shape[1]
    rows = batch * seq
    rows_per_seg = rows // mod.shape[0]
    d_a = wts["d_a"]
    d_b = wts["d_b"]
    lora_w = wts["lora_w"]
    lora_g = wts["lora_g"]
    c_xb = 3 * d_a
    c_gates = c_xb + d_b
    c_dw = c_gates + 2 * d
    c_dg = c_dw + LANES
    assert lora_w == HEAD and wts["lora_a"] == HEAD and lora_g == LANES
    assert seq % SCAN_TILE == 0 and SCAN_TILE % row_len == 0 and row_len & (row_len - 1) == 0

    tm = _pick(rows_per_seg, (1024, 512, 256))
    n_in = wts["w_in"].shape[1]
    proj = _inproj(x, mod, wts["norm_mix_g"], wts["w_in"], rows_per_seg, tm, _pick(n_in, (1280, 768, 512, 256)))

    outs = _scan(proj, c_dw // LANES, wts["rkv_conv_w"], wts["rkv_conv_b"], wts["decay_up"], wts["decay_base"],
                 wts["iclr_up"], wts["iclr_base"], wts["k_k"], wts["k_a"], wts["r_k"], wts["ones_bd"], wts["tri"],
                 s0, batch=batch, seq=seq, d_a=d_a, row_len=row_len, emit_state=emit_state)
    ys, bonus, states = outs[:2], outs[2], list(outs[3:])

    assert c_xb % d_b == 0
    z = _chan_dft(proj, wts["chan_cs"], c_xb // d_b, d_b, _pick(rows, (512, 256)))
    ct, st = wts["time_cs"][seq]
    yb = _time_dft(z, ct, st, batch, seq, _pick(seq, (512, 256)))

    x1 = _mix_out(ys[0], ys[1], bonus, c_dg // lora_g, wts["gate_up"], wts["lnx_g"], wts["lnx_b"], wts["avg_bd"],
                  yb, proj, c_gates, wts["w_out_a"], wts["w_fourier"], wts["w_out"], x, mod, rows_per_seg, 256)

    act = _ffn_in(x1, mod, wts["norm_ffn_g"], wts["ffn_w_in"], wts["ffn_conv_w"], wts["ffn_conv_b"],
                  rows_per_seg, row_len, tm, 512)
    y = _ffn_down(act, wts["ffn_w_down"], x1, mod, wts["final_norm_g"], rows_per_seg, 256)
    return y, states


def _tri_blockdiag(n, rev):
    i = np.arange(n)[:, None]
    j = np.arange(n)[None, :]
    same = (i // CHUNK) == (j // CHUNK)
    m = same & ((j >= i) if rev else (j <= i))
    return jnp.asarray(m.astype(np.float32), dtype=BF16)


def _blockdiag_ones(n):
    i = np.arange(n)[:, None] // HEAD
    j = np.arange(n)[None, :] // HEAD
    return jnp.asarray((i == j).astype(np.float32), dtype=BF16)


def kernel(x_prompt, x_sample, state_rwkv, c, c_ctx, ada_w, ada_b, norm_mix_g, w_in, rkv_conv_w, rkv_conv_b,
           decay_up, decay_base, iclr_up, iclr_base, gate_up, k_k, k_a, r_k, lnx_g, lnx_b, w_out_a, w_fourier,
           w_out, norm_ffn_g, ffn_w_in, ffn_conv_w, ffn_conv_b, ffn_w_down, final_norm_g):
    batch, ctx_len, d = x_prompt.shape
    dec_batch, dec_seq, _ = x_sample.shape
    depth = ada_w.shape[0]
    assert depth == 1
    d_a = w_out_a.shape[1]
    d_b = w_fourier.shape[1]
    d_ff = ffn_w_down.shape[1]
    ffp = -(-d_ff // 512) * 512
    lora_w = decay_up.shape[2]
    lora_a = iclr_up.shape[2]
    lora_g = gate_up.shape[1]
    group = d_b // N_GROUPS_B

    cvec = jnp.concatenate([c_ctx[None, :], c], axis=0).astype(F32)
    n_vec = cvec.shape[0]
    cvec = jnp.pad(cvec, ((0, 8 - n_vec), (0, 0)))
    mod = _modulation(cvec, ada_w[0].astype(F32), ada_b[0].astype(F32))[:n_vec].reshape(n_vec, 6, d)

    l = 0
    zpad_w = jnp.zeros((2, LANES - lora_w, d_a), F32)
    zpad_a = jnp.zeros((2, LANES - lora_a, d_a), F32)
    ffn_in = ffn_w_in[l]
    pad_ff = ((0, 0), (0, ffp - d_ff))
    cc, sc = _dft_mats(group)
    wts = {
        "d_a": d_a, "d_b": d_b, "lora_w": lora_w, "lora_a": lora_a, "lora_g": lora_g,
        "norm_mix_g": norm_mix_g[l], "w_in": w_in[l].astype(BF16),
        "rkv_conv_w": rkv_conv_w[l], "rkv_conv_b": rkv_conv_b[l].reshape(1, -1),
        "decay_up": jnp.concatenate([decay_up[l], zpad_w], axis=1),
        "iclr_up": jnp.concatenate([zpad_a, iclr_up[l]], axis=1),
        "decay_base": decay_base[l].reshape(2, 1, d_a), "iclr_base": iclr_base[l].reshape(2, 1, d_a),
        "k_k": k_k[l].reshape(1, d_a), "k_a": k_a[l].reshape(1, d_a), "r_k": r_k[l].reshape(1, d_a),
        "gate_up": gate_up[l].astype(BF16), "lnx_g": lnx_g[l], "lnx_b": lnx_b[l],
        "w_out_a": w_out_a[l].astype(BF16), "w_fourier": w_fourier[l].astype(BF16), "w_out": w_out[l].astype(BF16),
        "norm_ffn_g": norm_ffn_g[l],
        "ffn_w_in": tuple(jnp.pad(h.astype(BF16), pad_ff) for h in (ffn_in[:, :d_ff], ffn_in[:, d_ff:])),
        "ffn_conv_w": tuple(jnp.pad(h, pad_ff) for h in (ffn_conv_w[l][:, :d_ff], ffn_conv_w[l][:, d_ff:])),
        "ffn_conv_b": tuple(jnp.pad(h, pad_ff) for h in (ffn_conv_b[l][None, :d_ff], ffn_conv_b[l][None, d_ff:])),
        "ffn_w_down": jnp.pad(ffn_w_down[l].astype(BF16), ((0, ffp - d_ff), (0, 0))),
        "final_norm_g": final_norm_g,
        "ones_bd": _blockdiag_ones(LANES), "avg_bd": _blockdiag_ones(256),
        "tri": (_tri_blockdiag(SCAN_TILE, False), _tri_blockdiag(SCAN_TILE, True)),
        "chan_cs": jnp.concatenate([cc, sc], axis=1).astype(BF16),
        "time_cs": {n: tuple(m.astype(BF16) for m in _dft_mats(n)) for n in {ctx_len, dec_seq}},
    }

    y_ctx, st_ctx = _layer(x_prompt.reshape(batch * ctx_len, d), mod[:1], None, batch, ctx_len, ctx_len,
                           wts, emit_state=True)
    s0 = _pair_layout(jnp.swapaxes(state_rwkv[:, l].astype(F32), -1, -2))
    y_lat, _ = _layer(x_sample.reshape(dec_batch * dec_seq, d), mod[1:], s0, dec_batch, dec_seq, GRID_W,
                      wts, emit_state=False)

    new_state = jnp.stack(st_ctx, axis=1)[:, None].astype(x_prompt.dtype)
    return (y_ctx.reshape(batch, ctx_len, d), y_lat.reshape(dec_batch, dec_seq, d), new_state)
```

```python
import functools
import math

import numpy as np
import jax
import jax.numpy as jnp
from jax import lax
from jax.experimental import pallas as pl
from jax.experimental.pallas import tpu as pltpu

F32 = jnp.float32
BF16 = jnp.bfloat16

HEAD = 64
LANES = 128
MXU_COLS = 256
FFN_SUB_ROWS = 256
CHUNK = 64
SCAN_TILE = 256
SCAN_PAIRS = 8
GRID_W = 64
N_GROUPS_B = 4
RMS_EPS = 1e-6
GN_EPS = 64e-5
V7X_VMEM_BYTES = 64 * 1024 * 1024
VMEM_LIMIT = V7X_VMEM_BYTES - 8 * 1024 * 1024
RESIDENT_DOUBLE_BUFFER_MAX = 4 * 1024 * 1024

NN = ((1,), (0,))
NT = ((1,), (1,))


def _cparams(sem):
    return pltpu.CompilerParams(dimension_semantics=sem, vmem_limit_bytes=VMEM_LIMIT)


def _dot(a, b, dims=NN):
    return lax.dot_general(a, b, (dims, ((), ())), preferred_element_type=F32)


def _split2(x):
    hi = x.astype(BF16)
    lo = (x - hi.astype(F32)).astype(BF16)
    return hi, lo


def _split3(x):
    hi = x.astype(BF16)
    r1 = x - hi.astype(F32)
    mid = r1.astype(BF16)
    lo = (r1 - mid.astype(F32)).astype(BF16)
    return hi, mid, lo


def _dot_split_rhs(a, b):
    ab = a.astype(BF16)
    bh, bl = _split2(b)
    return _dot(jnp.concatenate([ab, ab], axis=1), jnp.concatenate([bh, bl], axis=0))


def _dot_exact_rhs(a, b_bf16, dims=NN):
    ah, al = _split2(a)
    return _dot(ah, b_bf16, dims) + _dot(al, b_bf16, dims)


def _sigmoid(x):
    return 1.0 / (1.0 + jnp.exp(-x))


def _pick(n, prefs):
    for p in prefs:
        if n % p == 0:
            return p
    raise ValueError(f"no tile in {prefs} divides {n}")


def _mod_kernel(c_ref, w_ref, b_ref, o_ref):
    c = c_ref[...]
    s = c * _sigmoid(c)
    o_ref[...] = jnp.dot(s, w_ref[...], preferred_element_type=F32,
                         precision=lax.Precision.HIGHEST) + b_ref[...]


def _modulation(cvec, ada_w, ada_b):
    d, n = ada_w.shape
    tn = _pick(n, (2048, 1024, 512))
    return pl.pallas_call(
        _mod_kernel,
        grid=(n // tn,),
        in_specs=[pl.BlockSpec((8, d), lambda j: (0, 0)),
                  pl.BlockSpec((d, tn), lambda j: (0, j)),
                  pl.BlockSpec((1, tn), lambda j: (0, j))],
        out_specs=pl.BlockSpec((8, tn), lambda j: (0, j)),
        out_shape=jax.ShapeDtypeStruct((8, n), F32),
        compiler_params=_cparams(("arbitrary",)),
        name="modulation",
    )(cvec, ada_w, ada_b.reshape(1, n))


def _rms_mod(x, g, shift, scale):
    ms = jnp.mean(x * x, axis=-1, keepdims=True)
    y = x * lax.rsqrt(ms + RMS_EPS) * g
    return y * (1.0 + scale) + shift


def _inproj_kernel(x_ref, mod_ref, g_ref, w_ref, o_ref, h_ref):
    @pl.when(pl.program_id(1) == 0)
    def _():
        h = _rms_mod(x_ref[...], g_ref[...], mod_ref[0:1, :], mod_ref[1:2, :])
        h_ref[...] = h.astype(BF16)

    o_ref[...] = jnp.dot(h_ref[...], w_ref[...], preferred_element_type=F32)


def _inproj(x, mod, g, w, rows_per_seg, tm, tn):
    r, d = x.shape
    n = w.shape[1]
    tps = rows_per_seg // tm
    return pl.pallas_call(
        _inproj_kernel,
        grid=(r // tm, n // tn),
        in_specs=[pl.BlockSpec((tm, d), lambda i, j: (i, 0)),
                  pl.BlockSpec((None, 6, d), lambda i, j: (i // tps, 0, 0)),
                  pl.BlockSpec((1, d), lambda i, j: (0, 0)),
                  pl.BlockSpec((d, tn), lambda i, j: (0, j))],
        out_specs=pl.BlockSpec((tm, tn), lambda i, j: (i, j)),
        out_shape=jax.ShapeDtypeStruct((r, n), F32),
        scratch_shapes=[pltpu.VMEM((tm, d), BF16)],
        compiler_params=_cparams(("parallel", "arbitrary")),
        name="inproj",
    )(x, mod, g.reshape(1, d), w)


def _conv3(x, w_ref, b_ref, pos, row_len):
    n = x.shape[0]
    prev = jnp.where(pos == 0, 0.0, pltpu.roll(x, 1, 0))
    nxt = jnp.where(pos == row_len - 1, 0.0, pltpu.roll(x, n - 1, 0))
    return prev * w_ref[0:1, :] + x * w_ref[1:2, :] + nxt * w_ref[2:3, :] + b_ref[...]


def _pair_diag(x, lane_lo):
    return jnp.concatenate([jnp.where(lane_lo, x, 0.0), jnp.where(lane_lo, 0.0, x)], axis=0)


def _scan_kernel(*refs, row_len, has_init, emit_state, n_pairs, shared_tile):
    it = iter(refs)
    take = lambda n: [next(it) for _ in range(n)]
    in_f = take(4)
    in_r = take(4)
    cwr_ref, cwk_ref, cwv_ref, cbr_ref, cbk_ref, cbv_ref = take(6)
    decup_refs = take(2)
    decb_refs = take(2)
    iclup_refs = take(2)
    iclb_refs = take(2)
    kkw_ref, kaw_ref, rkw_ref, ones_ref = take(4)
    tri_refs = take(2)
    s0_refs = take(2) if has_init else None
    y_refs = take(2)
    bonus_ref = next(it)
    sfin_refs = take(2) if emit_state else None
    s_ref = next(it)

    t = pl.program_id(2)
    n_t = pl.num_programs(2)
    tt, width = in_f[0].shape

    @pl.when(t == 0)
    def _():
        for dr in range(2):
            if has_init:
                s_ref[dr] = s0_refs[dr][...]
            else:
                s_ref[dr] = jnp.zeros(s_ref.shape[1:], F32)

    ones_bd = ones_ref[...]

    def b16(x):
        return x.astype(BF16)

    def head_sum(x):
        xb = b16(x)
        blk = ones_bd.shape[0]
        return jnp.concatenate([_dot(xb[:, j * blk:(j + 1) * blk], ones_bd) for j in range(width // blk)], axis=1)

    pos = lax.broadcasted_iota(jnp.int32, (tt, width), 0) & (row_len - 1)
    row = lax.broadcasted_iota(jnp.int32, (CHUNK, LANES), 0)
    lane = lax.broadcasted_iota(jnp.int32, (CHUNK, LANES), 1)
    col = lane & (HEAD - 1)
    lane_lo = lane < HEAD
    same16 = (row >> 4) == (col >> 4)
    same32 = (row >> 5) == (col >> 5)
    off16 = jnp.logical_and(same32, jnp.logical_not(same16))
    eye = jnp.where(col == row, 1.0, 0.0).astype(F32)
    eye_t = (lax.broadcasted_iota(jnp.int32, (LANES, LANES), 0)
             == lax.broadcasted_iota(jnp.int32, (LANES, LANES), 1)).astype(BF16)

    def pd(x):
        return b16(_pair_diag(x, lane_lo))

    def mm(p, q_pd):
        return _dot(b16(p), q_pd)

    def tokens(refs_in, with_bonus):
        rp_ref, kp_ref, vp_ref, _ = refs_in
        r = _conv3(rp_ref[...], cwr_ref, cbr_ref, pos, row_len)
        k = _conv3(kp_ref[...], cwk_ref, cbk_ref, pos, row_len)
        v = _conv3(vp_ref[...], cwv_ref, cbv_ref, pos, row_len)
        kk = k * kkw_ref[...]
        kk = kk * lax.rsqrt(head_sum(kk * kk) + 1e-12)
        if with_bonus:
            bonus_ref[...] = head_sum(r * k * rkw_ref[...]) * v
        return dict(r=r, k=k, v=v, kk=kk)

    n_chunks = tt // CHUNK

    def direction(dr, tok_fn, dwda_ref):
        rev = dr == 1
        strict = (col > row) if rev else (col < row)
        incl = (col >= row) if rev else (col <= row)
        order = list(range(n_chunks - 1, -1, -1) if rev else range(n_chunks))
        last_row = 0 if rev else CHUNK - 1
        streams = [(c, p) for c in order for p in range(n_pairs)]
        w = {}
        q = {}
        state = {}

        def prep_tile():
            w.update(tok_fn())
            dwda = dwda_ref[...]
            w_logit = decb_refs[dr][...] + _dot_split_rhs(jnp.tanh(dwda), decup_refs[dr][...])
            w["logw"] = (-math.exp(-0.5)) * _sigmoid(w_logit)
            a = _sigmoid(iclb_refs[dr][...] + _dot_split_rhs(dwda, iclup_refs[dr][...]))
            w["kd"] = w["k"] * (1.0 + (a - 1.0) * kaw_ref[...])
            w["bb"] = w["kk"] * a
            l1, l2 = _split2(w["logw"])
            tri = tri_refs[dr][...]
            w["cum"] = _dot(tri, l1) + _dot(tri, l2)
            for p in range(n_pairs):
                state[p] = s_ref[dr, p]

        def prep_chunk(c):
            def run():
                sl = slice(c * CHUNK, (c + 1) * CHUNK)
                cin = w["cum"][sl]
                c_end = cin[last_row:last_row + 1, :]
                g_inv = jnp.exp(-cin)
                g_rat = jnp.exp(c_end - cin)
                rt = w["r"][sl] * jnp.exp(cin)
                kkt = w["kk"][sl] * jnp.exp(cin - w["logw"][sl])
                kt = w["kd"][sl] * g_inv
                bt = w["bb"][sl] * g_inv
                kh = w["kd"][sl] * g_rat
                bh = w["bb"][sl] * g_rat
                vc = w["v"][sl]
                for p in range(n_pairs):
                    ls = slice(p * LANES, (p + 1) * LANES)
                    bk_t = b16(jnp.concatenate([bh[:, ls], kh[:, ls]], axis=0).T)
                    ce_t = jnp.broadcast_to(c_end[:, ls], (LANES, LANES)).T
                    g_col = jnp.exp(jnp.where(lane_lo, ce_t[:HEAD], ce_t[HEAD:]))
                    q[c, p] = dict(sl=sl, ls=ls, g_col=g_col, rt=b16(rt[:, ls]), kkt=kkt[:, ls], kt=kt[:, ls],
                                   bt=bt[:, ls], bk_t=bk_t, vc=vc[:, ls])
            return run

        def per_stream(fn):
            def run():
                for st in streams:
                    fn(q[st])
            return run

        def s_scores(d):
            x = jnp.concatenate([b16(d["kkt"]), d["rt"]], axis=0)
            sbk = _dot(x, jnp.concatenate([pd(d["bt"]), pd(d["kt"])], axis=0), NT)
            sb = sbk[:, :LANES]
            sk = sbk[:, LANES:]
            d["a_b"] = jnp.where(strict, sb[:CHUNK], 0.0)
            d["a_k"] = jnp.where(strict, sk[:CHUNK], 0.0)
            d["m_bk"] = b16(jnp.concatenate([jnp.where(incl, sb[CHUNK:], 0.0),
                                             jnp.where(incl, sk[CHUNK:], 0.0)], axis=1))
            d["n0"] = jnp.where(same16, d["a_b"], 0.0)
            d["vpd"] = pd(d["vc"])

        def s_n2(d):
            d["n2"] = mm(d["n0"], pd(d["n0"]))
            d["akv"] = mm(d["a_k"], d["vpd"])

        def s_n4(d):
            t1 = eye - d["n0"]
            both = _dot(b16(jnp.concatenate([d["n2"], t1], axis=0)), pd(d["n2"]))
            d["n4"] = both[:CHUNK]
            d["t"] = t1 + both[CHUNK:]

        def s_n8(d):
            both = _dot(b16(jnp.concatenate([d["n4"], d["t"]], axis=0)), pd(d["n4"]))
            d["n8"] = both[:CHUNK]
            d["t"] = d["t"] + both[CHUNK:]

        def s_t16(d):
            d["t"] = d["t"] + mm(d["t"], pd(d["n8"]))

        def s_p32(d):
            d["p"] = mm(d["t"], pd(jnp.where(off16, d["a_b"], 0.0)))

        def s_t(d):
            d["t"] = d["t"] - mm(d["p"], pd(d["t"]))

        def s_p64(d):
            d["p"] = mm(d["t"], pd(jnp.where(same32, 0.0, d["a_b"])))

        def s_w(d):
            w12 = _dot(b16(d["t"]), jnp.concatenate([pd(d["kkt"]), pd(d["akv"])], axis=1))
            d["w1r"] = jnp.concatenate([b16(w12[:, :LANES]), d["rt"]], axis=0)
            d["w2"] = w12[:, LANES:]

        def chain(c):
            def run():
                for p in range(n_pairs):
                    d = q[c, p]
                    xs = _dot(d["w1r"], pd(state[p]))
                    d["u"] = -(xs[:CHUNK] + d["w2"])
                    d["ys"] = xs[CHUNK:]
                    full = _dot(d["bk_t"], b16(jnp.concatenate([d["u"], d["vc"]], axis=0)))
                    state[p] = state[p] * d["g_col"] + jnp.where(lane_lo, full[:CHUNK], full[CHUNK:])
            return run

        def s_y(d):
            y = d["ys"] + _dot(d["m_bk"], jnp.concatenate([pd(d["u"]), d["vpd"]], axis=0))
            y_refs[dr][d["sl"], d["ls"]] = y.astype(y_refs[dr].dtype)

        def finish():
            for p in range(n_pairs):
                s_ref[dr, p] = state[p]
            if emit_state:
                @pl.when(t == n_t - 1)
                def _():
                    for p in range(n_pairs):
                        h1, h2, h3 = _split3(state[p])
                        tr = _dot(eye_t, h1, NT) + (_dot(eye_t, h2, NT) + _dot(eye_t, h3, NT))
                        sfin_refs[dr][2 * p] = tr[:HEAD]
                        sfin_refs[dr][2 * p + 1] = tr[HEAD:]

        prep = [prep_tile] + [prep_chunk(c) for c in order]
        work = [per_stream(f) for f in (s_scores, s_n2, s_n4, s_n8, s_t16, s_p32, s_t, s_p64, s_t, s_w)]
        work += [chain(c) for c in order] + [per_stream(s_y), finish]
        return prep, work

    tok_cache = {}

    def tok_f():
        tok_cache["f"] = tokens(in_f, True)
        return tok_cache["f"]

    def tok_r():
        return tok_cache["f"] if shared_tile else tokens(in_r, False)

    prep_f, work_f = direction(0, tok_f, in_f[3])
    prep_r, work_r = direction(1, tok_r, in_r[3])
    for fn in prep_f:
        fn()
    seq_a = work_f
    seq_b = prep_r + work_r
    for i in range(max(len(seq_a), len(seq_b))):
        if i < len(seq_a):
            seq_a[i]()
        if i < len(seq_b):
            seq_b[i]()


def _scan(proj, dwda_blk, conv_w, conv_b, dec_up, dec_base, icl_up, icl_base, k_k, k_a, r_k,
          ones_bd, tri, s0, *, batch, seq, d_a, row_len, emit_state):
    n_t = seq // SCAN_TILE
    n_p = d_a // LANES
    pps = SCAN_PAIRS
    assert n_p % pps == 0
    n_g = n_p // pps
    width = pps * LANES
    tt = SCAN_TILE
    rows = batch * seq

    def tile(rev):
        return (lambda b, t: b * n_t + (n_t - 1 - t)) if rev else (lambda b, t: b * n_t + t)

    def colspec(base, rev):
        ti = tile(rev)
        return pl.BlockSpec((tt, width), lambda b, g, t: (ti(b, t), base + g))

    def tokspecs(rev):
        ti = tile(rev)
        return [colspec(0, rev), colspec(n_g, rev), colspec(2 * n_g, rev),
                pl.BlockSpec((tt, LANES), lambda b, g, t: (ti(b, t), dwda_blk))]

    def wspec(nrow, base=0):
        return pl.BlockSpec((nrow, width), lambda b, g, t, base=base: (0, base + g))

    const2 = lambda shape: pl.BlockSpec(shape, lambda b, g, t: (0, 0))
    in_specs = (tokspecs(False) + tokspecs(True)
                + [wspec(3, 0), wspec(3, n_g), wspec(3, 2 * n_g), wspec(1, 0), wspec(1, n_g), wspec(1, 2 * n_g)]
                + [wspec(LANES)] * 2 + [wspec(1)] * 2 + [wspec(LANES)] * 2 + [wspec(1)] * 2
                + [wspec(1)] * 3 + [const2(ones_bd.shape)] + [const2((tt, tt))] * 2)
    args = ([proj] * 8 + [conv_w] * 3 + [conv_b] * 3
            + [dec_up[0], dec_up[1], dec_base[0], dec_base[1], icl_up[0], icl_up[1], icl_base[0], icl_base[1]]
            + [k_k, k_a, r_k, ones_bd, tri[0], tri[1]])
    has_init = s0 is not None
    if has_init:
        for dr in range(2):
            in_specs.append(pl.BlockSpec((None, None, pps, HEAD, LANES), lambda b, g, t, dr=dr: (b, dr, g, 0, 0)))
            args.append(s0)
    yspec = lambda rev: colspec(0, rev)
    out_specs = [yspec(False), yspec(True), yspec(False)]
    out_shape = [jax.ShapeDtypeStruct((rows, d_a), BF16)] * 2 + [jax.ShapeDtypeStruct((rows, d_a), F32)]
    if emit_state:
        out_specs += [pl.BlockSpec((None, 2 * pps, HEAD, HEAD), lambda b, g, t: (b, g, 0, 0))] * 2
        out_shape += [jax.ShapeDtypeStruct((batch, 2 * n_p, HEAD, HEAD), F32)] * 2
    kern = functools.partial(_scan_kernel, row_len=row_len, has_init=has_init, emit_state=emit_state,
                             n_pairs=pps, shared_tile=(n_t == 1))
    return pl.pallas_call(
        kern,
        grid=(batch, n_g, n_t),
        in_specs=in_specs,
        out_specs=out_specs,
        out_shape=out_shape,
        scratch_shapes=[pltpu.VMEM((2, pps, HEAD, LANES), F32)],
        compiler_params=_cparams(("parallel", "parallel", "arbitrary")),
        name="rwkv7_scan_bidir",
    )(*args)


def _chan_dft_kernel(x_ref, w_ref, o_ref):
    group = w_ref.shape[0]
    w = w_ref[...]
    for g in range(x_ref.shape[1] // group):
        xg = x_ref[:, g * group:(g + 1) * group].astype(BF16)
        o_ref[:, 2 * g * group:2 * (g + 1) * group] = jnp.dot(xg, w, preferred_element_type=F32).astype(BF16)


def _chan_dft(proj, w_cs, xb_blk, d_b, tm):
    r = proj.shape[0]
    group = w_cs.shape[0]
    return pl.pallas_call(
        _chan_dft_kernel,
        grid=(r // tm,),
        in_specs=[pl.BlockSpec((tm, d_b), lambda i: (i, xb_blk)),
                  pl.BlockSpec((group, 2 * group), lambda i: (0, 0))],
        out_specs=pl.BlockSpec((tm, 2 * d_b), lambda i: (i, 0)),
        out_shape=jax.ShapeDtypeStruct((r, 2 * d_b), BF16),
        compiler_params=_cparams(("parallel",)),
        name="fourier_channels",
    )(proj, w_cs)


def _time_dft_kernel(ct_ref, st_ref, z_ref, o_ref):
    ct = ct_ref[...]
    st = st_ref[...]
    group = o_ref.shape[1] // N_GROUPS_B
    for g in range(N_GROUPS_B):
        zc = z_ref[:, 2 * g * group:(2 * g + 1) * group]
        zs = z_ref[:, (2 * g + 1) * group:(2 * g + 2) * group]
        y = jnp.dot(ct, zc, preferred_element_type=F32) - jnp.dot(st, zs, preferred_element_type=F32)
        o_ref[:, g * group:(g + 1) * group] = y.astype(BF16)


def _time_dft(z, ct, st, batch, seq, tm):
    n_m = seq // tm
    d_b = z.shape[1] // 2
    z_mode = pl.Buffered(1) if seq * 2 * d_b * z.dtype.itemsize > RESIDENT_DOUBLE_BUFFER_MAX else None
    return pl.pallas_call(
        _time_dft_kernel,
        grid=(batch, n_m),
        in_specs=[pl.BlockSpec((tm, seq), lambda b, m: (m, 0)),
                  pl.BlockSpec((tm, seq), lambda b, m: (m, 0)),
                  pl.BlockSpec((seq, 2 * d_b), lambda b, m: (b, 0), pipeline_mode=z_mode)],
        out_specs=pl.BlockSpec((tm, d_b), lambda b, m: (b * n_m + m, 0)),
        out_shape=jax.ShapeDtypeStruct((batch * seq, d_b), BF16),
        compiler_params=_cparams(("parallel", "arbitrary")),
        name="fourier_positions",
    )(ct, st, z)


def _dft_mats(n):
    scale = 1.0 / math.sqrt(n)
    m = math.isqrt(n)
    if m * m != n or n <= 1024:
        i = lax.broadcasted_iota(jnp.int32, (n, n), 0)
        j = lax.broadcasted_iota(jnp.int32, (n, n), 1)
        ang = ((i * j) % n).astype(F32) * (2.0 * math.pi / n)
        return jnp.cos(ang) * scale, jnp.sin(ang) * scale
    t = lax.broadcasted_iota(jnp.int32, (m, n), 0)
    f = lax.broadcasted_iota(jnp.int32, (m, n), 1)
    a1 = ((m * t * f) % n).astype(F32) * (2.0 * math.pi / n)
    a2 = ((t * f) % n).astype(F32) * (2.0 * math.pi / n)
    c1, s1 = (jnp.cos(a1) * scale)[:, None, :], (jnp.sin(a1) * scale)[:, None, :]
    c2, s2 = jnp.cos(a2)[None, :, :], jnp.sin(a2)[None, :, :]
    return (c1 * c2 - s1 * s2).reshape(n, n), (s1 * c2 + c1 * s2).reshape(n, n)


def _mix_out_kernel(yf_ref, yr_ref, bonus_ref, dg_ref, gup_ref, lg_ref, lb_ref, avg_ref, yb_ref,
                    ga0_ref, ga1_ref, gb0_ref, gb1_ref, wa_ref, wf_ref, wo_ref, x_ref, mod_ref, o_ref, ya_ref):
    avg = avg_ref[...]
    cw = avg.shape[0]
    sig = _sigmoid(dg_ref[...]).astype(BF16)
    for j in range(ya_ref.shape[1] // cw):
        cols = slice(j * cw, (j + 1) * cw)
        y = yf_ref[:, cols].astype(F32) + yr_ref[:, cols].astype(F32)
        mu = _dot_exact_rhs(y, avg) * (1.0 / HEAD)
        d = y - mu
        var = _dot_exact_rhs(d * d, avg) * (1.0 / HEAD)
        yn = d * lax.rsqrt(var + GN_EPS) * lg_ref[:, cols] + lb_ref[:, cols]
        g = jnp.dot(sig, gup_ref[:, cols], preferred_element_type=F32)
        ya_ref[:, cols] = ((yn + bonus_ref[:, cols]) * g).astype(BF16)

    half = ga0_ref.shape[1]
    ya = ya_ref[...]
    yb = yb_ref[...]
    parts = []
    for q, (ga_ref, gb_ref) in enumerate(((ga0_ref, gb0_ref), (ga1_ref, gb1_ref))):
        cols = slice(q * half, (q + 1) * half)
        pa = jnp.dot(ya, wa_ref[:, cols], preferred_element_type=F32)
        pb = jnp.dot(yb, wf_ref[:, cols], preferred_element_type=F32)
        parts.append((_sigmoid(ga_ref[...]) * pa + _sigmoid(gb_ref[...]) * pb).astype(BF16))
    mixed = jnp.concatenate(parts, axis=1)
    p = jnp.dot(mixed, wo_ref[...], preferred_element_type=F32)
    o_ref[...] = x_ref[...] + mod_ref[2:3, :] * p


def _mix_out(yf, yr, bonus, dg_blk, gate_up, lnx_g, lnx_b, avg, yb, proj, c_gates, w_a, w_f, w_o, x, mod,
             rows_per_seg, tm):
    r, d = x.shape
    d_a = yf.shape[1]
    d_b = yb.shape[1]
    lg = gate_up.shape[0]
    cw = avg.shape[0]
    half = d // 2
    assert c_gates % half == 0
    gblk = c_gates // half
    tps = rows_per_seg // tm
    resident = lambda shape: pl.BlockSpec(shape, lambda i: (0, 0), pipeline_mode=pl.Buffered(1))
    gate = lambda k: pl.BlockSpec((tm, half), lambda i, k=k: (i, gblk + k))
    tile_a = pl.BlockSpec((tm, d_a), lambda i: (i, 0))
    vec_a = pl.BlockSpec((1, d_a), lambda i: (0, 0))
    return pl.pallas_call(
        _mix_out_kernel,
        grid=(r // tm,),
        in_specs=[tile_a, tile_a, tile_a,
                  pl.BlockSpec((tm, lg), lambda i: (i, dg_blk)),
                  resident((lg, d_a)), vec_a, vec_a, resident((cw, cw)),
                  pl.BlockSpec((tm, d_b), lambda i: (i, 0)),
                  gate(0), gate(1), gate(2), gate(3),
                  resident((d_a, d)), resident((d_b, d)), resident((d, d)),
                  pl.BlockSpec((tm, d), lambda i: (i, 0)),
                  pl.BlockSpec((None, 6, d), lambda i: (i // tps, 0, 0))],
        out_specs=pl.BlockSpec((tm, d), lambda i: (i, 0)),
        out_shape=jax.ShapeDtypeStruct((r, d), F32),
        scratch_shapes=[pltpu.VMEM((tm, d_a), BF16)],
        compiler_params=_cparams(("parallel",)),
        name="headnorm_merge_outproj",
    )(yf, yr, bonus, proj, gate_up, lnx_g.reshape(1, d_a), lnx_b.reshape(1, d_a), avg, yb,
      proj, proj, proj, proj, w_a, w_f, w_o, x, mod)


def _ffn_in_kernel(x_ref, mod_ref, g_ref, wg_ref, wv_ref, cwg_ref, cwv_ref, cbg_ref, cbv_ref, o_ref, h_ref,
                   *, row_len):
    @pl.when(pl.program_id(1) == 0)
    def _():
        h = _rms_mod(x_ref[...], g_ref[...], mod_ref[3:4, :], mod_ref[4:5, :])
        h_ref[...] = h.astype(BF16)

    tm, tn = o_ref.shape
    sub_n = MXU_COLS
    sub_m = min(tm, FFN_SUB_ROWS)
    assert sub_m % row_len == 0
    pos = lax.broadcasted_iota(jnp.int32, (sub_m, sub_n), 0) & (row_len - 1)
    blocks = [(slice(m * sub_m, (m + 1) * sub_m), slice(q * sub_n, (q + 1) * sub_n))
              for m in range(tm // sub_m) for q in range(tn // sub_n)]

    def matmuls(blk):
        rows, cols = blk
        h = h_ref[rows, :]
        return (jnp.dot(h, wg_ref[:, cols], preferred_element_type=F32),
                jnp.dot(h, wv_ref[:, cols], preferred_element_type=F32))

    def epilogue(blk, ug, uv):
        rows, cols = blk
        ug = _conv3(ug, cwg_ref.at[:, cols], cbg_ref.at[:, cols], pos, row_len)
        uv = _conv3(uv, cwv_ref.at[:, cols], cbv_ref.at[:, cols], pos, row_len)
        o_ref[rows, cols] = (ug * _sigmoid(ug) * uv).astype(BF16)

    cur = matmuls(blocks[0])
    for i, blk in enumerate(blocks):
        nxt = matmuls(blocks[i + 1]) if i + 1 < len(blocks) else None
        epilogue(blk, *cur)
        cur = nxt


def _ffn_in(x, mod, g, w_gv, cw_gv, cb_gv, rows_per_seg, row_len, tm, tn):
    r, d = x.shape
    ffp = w_gv[0].shape[1]
    tps = rows_per_seg // tm
    wspec = pl.BlockSpec((d, tn), lambda i, j: (0, j))
    cwspec = pl.BlockSpec((3, tn), lambda i, j: (0, j))
    cbspec = pl.BlockSpec((1, tn), lambda i, j: (0, j))
    return pl.pallas_call(
        functools.partial(_ffn_in_kernel, row_len=row_len),
        grid=(r // tm, ffp // tn),
        in_specs=[pl.BlockSpec((tm, d), lambda i, j: (i, 0)),
                  pl.BlockSpec((None, 6, d), lambda i, j: (i // tps, 0, 0)),
                  pl.BlockSpec((1, d), lambda i, j: (0, 0)),
                  wspec, wspec, cwspec, cwspec, cbspec, cbspec],
        out_specs=pl.BlockSpec((tm, tn), lambda i, j: (i, j)),
        out_shape=jax.ShapeDtypeStruct((r, ffp), BF16),
        scratch_shapes=[pltpu.VMEM((tm, d), BF16)],
        compiler_params=_cparams(("parallel", "arbitrary")),
        name="ffn_in_conv_gate",
    )(x, mod, g.reshape(1, d), *w_gv, *cw_gv, *cb_gv)


def _ffn_down_kernel(a_ref, w_ref, x_ref, mod_ref, g_ref, o_ref):
    x2 = x_ref[...] + mod_ref[5:6, :] * jnp.dot(a_ref[...], w_ref[...], preferred_element_type=F32)
    ms = jnp.mean(x2 * x2, axis=-1, keepdims=True)
    o_ref[...] = x2 * lax.rsqrt(ms + RMS_EPS) * g_ref[...]


def _ffn_down(act, w, x, mod, g, rows_per_seg, tm):
    r, d = x.shape
    ffp = act.shape[1]
    tps = rows_per_seg // tm
    return pl.pallas_call(
        _ffn_down_kernel,
        grid=(r // tm,),
        in_specs=[pl.BlockSpec((tm, ffp), lambda i: (i, 0)),
                  pl.BlockSpec((ffp, d), lambda i: (0, 0), pipeline_mode=pl.Buffered(1)),
                  pl.BlockSpec((tm, d), lambda i: (i, 0)),
                  pl.BlockSpec((None, 6, d), lambda i: (i // tps, 0, 0)),
                  pl.BlockSpec((1, d), lambda i: (0, 0))],
        out_specs=pl.BlockSpec((tm, d), lambda i: (i, 0)),
        out_shape=jax.ShapeDtypeStruct((r, d), F32),
        compiler_params=_cparams(("parallel",)),
        name="ffn_down_final_norm",
    )(act, w, x, mod, g.reshape(1, d))


def _pair_layout(s):
    lead = s.shape[:-3]
    h = s.shape[-3]
    s = s.reshape(lead + (h // 2, 2, HEAD, HEAD))
    s = jnp.swapaxes(s, -3, -2)
    return s.reshape(lead + (h // 2, HEAD, 2 * HEAD))


def _layer(x, mod, s0, batch, seq, row_len, wts, emit_state):
    d = x.shape[1]
    rows = batch * seq
    rows_per_seg = rows // mod.shape[0]
    d_a = wts["d_a"]
    d_b = wts["d_b"]
    lora_w = wts["lora_w"]
    lora_g = wts["lora_g"]
    c_xb = 3 * d_a
    c_gates = c_xb + d_b
    c_dw = c_gates + 2 * d
    c_dg = c_dw + LANES
    assert lora_w == HEAD and wts["lora_a"] == HEAD and lora_g == LANES
    assert seq % SCAN_TILE == 0 and SCAN_TILE % row_len == 0 and row_len & (row_len - 1) == 0

    tm = _pick(rows_per_seg, (1024, 512, 256))
    n_in = wts["w_in"].shape[1]
    proj = _inproj(x, mod, wts["norm_mix_g"], wts["w_in"], rows_per_seg, tm, _pick(n_in, (1280, 768, 512, 256)))

    outs = _scan(proj, c_dw // LANES, wts["rkv_conv_w"], wts["rkv_conv_b"], wts["decay_up"], wts["decay_base"],
                 wts["iclr_up"], wts["iclr_base"], wts["k_k"], wts["k_a"], wts["r_k"], wts["avg_bd"], wts["tri"],
                 s0, batch=batch, seq=seq, d_a=d_a, row_len=row_len, emit_state=emit_state)
    ys, bonus, states = outs[:2], outs[2], list(outs[3:])

    assert c_xb % d_b == 0
    z = _chan_dft(proj, wts["chan_cs"], c_xb // d_b, d_b, _pick(rows, (512, 256)))
    ct, st = wts["time_cs"][seq]
    yb = _time_dft(z, ct, st, batch, seq, _pick(seq, (512, 256)))

    x1 = _mix_out(ys[0], ys[1], bonus, c_dg // lora_g, wts["gate_up"], wts["lnx_g"], wts["lnx_b"], wts["avg_bd"],
                  yb, proj, c_gates, wts["w_out_a"], wts["w_fourier"], wts["w_out"], x, mod, rows_per_seg, 256)

    act = _ffn_in(x1, mod, wts["norm_ffn_g"], wts["ffn_w_in"], wts["ffn_conv_w"], wts["ffn_conv_b"],
                  rows_per_seg, row_len, tm, 512)
    y = _ffn_down(act, wts["ffn_w_down"], x1, mod, wts["final_norm_g"], rows_per_seg, 256)
    return y, states


def _tri_blockdiag(n, rev):
    i = np.arange(n)[:, None]
    j = np.arange(n)[None, :]
    same = (i // CHUNK) == (j // CHUNK)
    m = same & ((j >= i) if rev else (j <= i))
    return jnp.asarray(m.astype(np.float32), dtype=BF16)


def _blockdiag_ones(n):
    i = np.arange(n)[:, None] // HEAD
    j = np.arange(n)[None, :] // HEAD
    return jnp.asarray((i == j).astype(np.float32), dtype=BF16)


def kernel(x_prompt, x_sample, state_rwkv, c, c_ctx, ada_w, ada_b, norm_mix_g, w_in, rkv_conv_w, rkv_conv_b,
           decay_up, decay_base, iclr_up, iclr_base, gate_up, k_k, k_a, r_k, lnx_g, lnx_b, w_out_a, w_fourier,
           w_out, norm_ffn_g, ffn_w_in, ffn_conv_w, ffn_conv_b, ffn_w_down, final_norm_g):
    batch, ctx_len, d = x_prompt.shape
    dec_batch, dec_seq, _ = x_sample.shape
    depth = ada_w.shape[0]
    assert depth == 1
    d_a = w_out_a.shape[1]
    d_b = w_fourier.shape[1]
    d_ff = ffn_w_down.shape[1]
    ffp = -(-d_ff // 512) * 512
    lora_w = decay_up.shape[2]
    lora_a = iclr_up.shape[2]
    lora_g = gate_up.shape[1]
    group = d_b // N_GROUPS_B

    cvec = jnp.concatenate([c_ctx[None, :], c], axis=0).astype(F32)
    n_vec = cvec.shape[0]
    cvec = jnp.pad(cvec, ((0, 8 - n_vec), (0, 0)))
    mod = _modulation(cvec, ada_w[0].astype(F32), ada_b[0].astype(F32))[:n_vec].reshape(n_vec, 6, d)

    l = 0
    zpad_w = jnp.zeros((2, LANES - lora_w, d_a), F32)
    zpad_a = jnp.zeros((2, LANES - lora_a, d_a), F32)
    ffn_in = ffn_w_in[l]
    pad_ff = ((0, 0), (0, ffp - d_ff))
    cc, sc = _dft_mats(group)
    wts = {
        "d_a": d_a, "d_b": d_b, "lora_w": lora_w, "lora_a": lora_a, "lora_g": lora_g,
        "norm_mix_g": norm_mix_g[l], "w_in": w_in[l].astype(BF16),
        "rkv_conv_w": rkv_conv_w[l], "rkv_conv_b": rkv_conv_b[l].reshape(1, -1),
        "decay_up": jnp.concatenate([decay_up[l], zpad_w], axis=1),
        "iclr_up": jnp.concatenate([zpad_a, iclr_up[l]], axis=1),
        "decay_base": decay_base[l].reshape(2, 1, d_a), "iclr_base": iclr_base[l].reshape(2, 1, d_a),
        "k_k": k_k[l].reshape(1, d_a), "k_a": k_a[l].reshape(1, d_a), "r_k": r_k[l].reshape(1, d_a),
        "gate_up": gate_up[l].astype(BF16), "lnx_g": lnx_g[l], "lnx_b": lnx_b[l],
        "w_out_a": w_out_a[l].astype(BF16), "w_fourier": w_fourier[l].astype(BF16), "w_out": w_out[l].astype(BF16),
        "norm_ffn_g": norm_ffn_g[l],
        "ffn_w_in": tuple(jnp.pad(h.astype(BF16), pad_ff) for h in (ffn_in[:, :d_ff], ffn_in[:, d_ff:])),
        "ffn_conv_w": tuple(jnp.pad(h, pad_ff) for h in (ffn_conv_w[l][:, :d_ff], ffn_conv_w[l][:, d_ff:])),
        "ffn_conv_b": tuple(jnp.pad(h, pad_ff) for h in (ffn_conv_b[l][None, :d_ff], ffn_conv_b[l][None, d_ff:])),
        "ffn_w_down": jnp.pad(ffn_w_down[l].astype(BF16), ((0, ffp - d_ff), (0, 0))),
        "final_norm_g": final_norm_g,
        "avg_bd": _blockdiag_ones(MXU_COLS),
        "tri": (_tri_blockdiag(SCAN_TILE, False), _tri_blockdiag(SCAN_TILE, True)),
        "chan_cs": jnp.concatenate([cc, sc], axis=1).astype(BF16),
        "time_cs": {n: tuple(m.astype(BF16) for m in _dft_mats(n)) for n in {ctx_len, dec_seq}},
    }

    y_ctx, st_ctx = _layer(x_prompt.reshape(batch * ctx_len, d), mod[:1], None, batch, ctx_len, ctx_len,
                           wts, emit_state=True)
    s0 = _pair_layout(jnp.swapaxes(state_rwkv[:, l].astype(F32), -1, -2))
    y_lat, _ = _layer(x_sample.reshape(dec_batch * dec_seq, d), mod[1:], s0, dec_batch, dec_seq, GRID_W,
                      wts, emit_state=False)

    new_state = jnp.stack(st_ctx, axis=1)[:, None].astype(x_prompt.dtype)
    return (y_ctx.reshape(batch, ctx_len, d), y_lat.reshape(dec_batch, dec_seq, d), new_state)
```

```python
import functools
import math

import numpy as np
import jax
import jax.numpy as jnp
from jax import lax
from jax.experimental import pallas as pl
from jax.experimental.pallas import tpu as pltpu

F32 = jnp.float32
BF16 = jnp.bfloat16

HEAD = 64
LANES = 128
MXU_COLS = 256
FFN_SUB_ROWS = 256
CHUNK = 64
SCAN_TILE = 256
SCAN_PAIRS = 8
GRID_W = 64
N_GROUPS_B = 4
RMS_EPS = 1e-6
GN_EPS = 64e-5
V7X_VMEM_BYTES = 64 * 1024 * 1024
VMEM_LIMIT = V7X_VMEM_BYTES - 8 * 1024 * 1024
RESIDENT_DOUBLE_BUFFER_MAX = 4 * 1024 * 1024

NN = ((1,), (0,))
NT = ((1,), (1,))


def _cparams(sem):
    return pltpu.CompilerParams(dimension_semantics=sem, vmem_limit_bytes=VMEM_LIMIT)


def _dot(a, b, dims=NN):
    return lax.dot_general(a, b, (dims, ((), ())), preferred_element_type=F32)


def _split2(x):
    hi = x.astype(BF16)
    lo = (x - hi.astype(F32)).astype(BF16)
    return hi, lo


def _split3(x):
    hi = x.astype(BF16)
    r1 = x - hi.astype(F32)
    mid = r1.astype(BF16)
    lo = (r1 - mid.astype(F32)).astype(BF16)
    return hi, mid, lo


def _dot_split_rhs(a, b):
    ab = a.astype(BF16)
    bh, bl = _split2(b)
    return _dot(jnp.concatenate([ab, ab], axis=1), jnp.concatenate([bh, bl], axis=0))


def _sigmoid(x):
    return 1.0 / (1.0 + jnp.exp(-x))


def _pick(n, prefs):
    for p in prefs:
        if n % p == 0:
            return p
    raise ValueError(f"no tile in {prefs} divides {n}")


def _mod_kernel(c_ref, w_ref, b_ref, o_ref):
    c = c_ref[...]
    s = c * _sigmoid(c)
    o_ref[...] = jnp.dot(s, w_ref[...], preferred_element_type=F32,
                         precision=lax.Precision.HIGHEST) + b_ref[...]


def _modulation(cvec, ada_w, ada_b):
    d, n = ada_w.shape
    tn = _pick(n, (1024, 512))
    return pl.pallas_call(
        _mod_kernel,
        grid=(n // tn,),
        in_specs=[pl.BlockSpec((8, d), lambda j: (0, 0)),
                  pl.BlockSpec((d, tn), lambda j: (0, j)),
                  pl.BlockSpec((1, tn), lambda j: (0, j))],
        out_specs=pl.BlockSpec((8, tn), lambda j: (0, j)),
        out_shape=jax.ShapeDtypeStruct((8, n), F32),
        compiler_params=_cparams(("arbitrary",)),
        name="modulation",
    )(cvec, ada_w, ada_b.reshape(1, n))


def _rms_mod(x, g, shift, scale):
    ms = jnp.mean(x * x, axis=-1, keepdims=True)
    y = x * lax.rsqrt(ms + RMS_EPS) * g
    return y * (1.0 + scale) + shift


def _inproj_kernel(x_ref, mod_ref, g_ref, w_ref, o_ref, h_ref):
    @pl.when(pl.program_id(1) == 0)
    def _():
        h = _rms_mod(x_ref[...], g_ref[...], mod_ref[0:1, :], mod_ref[1:2, :])
        h_ref[...] = h.astype(BF16)

    o_ref[...] = jnp.dot(h_ref[...], w_ref[...], preferred_element_type=F32)


def _inproj(x, mod, g, w, rows_per_seg, tm, tn):
    r, d = x.shape
    n = w.shape[1]
    tps = rows_per_seg // tm
    return pl.pallas_call(
        _inproj_kernel,
        grid=(r // tm, n // tn),
        in_specs=[pl.BlockSpec((tm, d), lambda i, j: (i, 0)),
                  pl.BlockSpec((None, 6, d), lambda i, j: (i // tps, 0, 0)),
                  pl.BlockSpec((1, d), lambda i, j: (0, 0)),
                  pl.BlockSpec((d, tn), lambda i, j: (0, j))],
        out_specs=pl.BlockSpec((tm, tn), lambda i, j: (i, j)),
        out_shape=jax.ShapeDtypeStruct((r, n), F32),
        scratch_shapes=[pltpu.VMEM((tm, d), BF16)],
        compiler_params=_cparams(("parallel", "arbitrary")),
        name="inproj",
    )(x, mod, g.reshape(1, d), w)


def _conv3(x, w_ref, b_ref, pos, row_len):
    n = x.shape[0]
    prev = jnp.where(pos == 0, 0.0, pltpu.roll(x, 1, 0))
    nxt = jnp.where(pos == row_len - 1, 0.0, pltpu.roll(x, n - 1, 0))
    return prev * w_ref[0:1, :] + x * w_ref[1:2, :] + nxt * w_ref[2:3, :] + b_ref[...]


def _pair_diag(x, lane_lo):
    return jnp.concatenate([jnp.where(lane_lo, x, 0.0), jnp.where(lane_lo, 0.0, x)], axis=0)


def _scan_kernel(*refs, row_len, has_init, emit_state, n_pairs, shared_tile):
    it = iter(refs)
    take = lambda n: [next(it) for _ in range(n)]
    in_f = take(4)
    in_r = take(4)
    cwr_ref, cwk_ref, cwv_ref, cbr_ref, cbk_ref, cbv_ref = take(6)
    decup_refs = take(2)
    decb_refs = take(2)
    iclup_refs = take(2)
    iclb_refs = take(2)
    kkw_ref, kaw_ref, rkw_ref, ones_ref = take(4)
    tri_refs = take(2)
    s0_refs = take(2) if has_init else None
    y_refs = take(2)
    bonus_ref = next(it)
    sfin_refs = take(2) if emit_state else None
    s_ref = next(it)

    t = pl.program_id(2)
    n_t = pl.num_programs(2)
    tt, width = in_f[0].shape

    @pl.when(t == 0)
    def _():
        for dr in range(2):
            if has_init:
                s_ref[dr] = s0_refs[dr][...]
            else:
                s_ref[dr] = jnp.zeros(s_ref.shape[1:], F32)

    ones_bd = ones_ref[...]

    def b16(x):
        return x.astype(BF16)

    def head_sum(x):
        xb = b16(x)
        blk = ones_bd.shape[0]
        return jnp.concatenate([_dot(xb[:, j * blk:(j + 1) * blk], ones_bd) for j in range(width // blk)], axis=1)

    pos = lax.broadcasted_iota(jnp.int32, (tt, width), 0) & (row_len - 1)
    row = lax.broadcasted_iota(jnp.int32, (CHUNK, LANES), 0)
    lane = lax.broadcasted_iota(jnp.int32, (CHUNK, LANES), 1)
    col = lane & (HEAD - 1)
    lane_lo = lane < HEAD
    same16 = (row >> 4) == (col >> 4)
    same32 = (row >> 5) == (col >> 5)
    off16 = jnp.logical_and(same32, jnp.logical_not(same16))
    eye = jnp.where(col == row, 1.0, 0.0).astype(F32)
    eye_t = (lax.broadcasted_iota(jnp.int32, (LANES, LANES), 0)
             == lax.broadcasted_iota(jnp.int32, (LANES, LANES), 1)).astype(BF16)

    def pd(x):
        return b16(_pair_diag(x, lane_lo))

    def mm(p, q_pd):
        return _dot(b16(p), q_pd)

    def tokens(refs_in, with_bonus):
        rp_ref, kp_ref, vp_ref, _ = refs_in
        r = _conv3(rp_ref[...], cwr_ref, cbr_ref, pos, row_len)
        k = _conv3(kp_ref[...], cwk_ref, cbk_ref, pos, row_len)
        v = _conv3(vp_ref[...], cwv_ref, cbv_ref, pos, row_len)
        kk = k * kkw_ref[...]
        kk = kk * lax.rsqrt(head_sum(kk * kk) + 1e-12)
        if with_bonus:
            bonus_ref[...] = head_sum(r * k * rkw_ref[...]) * v
        return dict(r=r, k=k, v=v, kk=kk)

    n_chunks = tt // CHUNK

    def direction(dr, tok_fn, dwda_ref):
        rev = dr == 1
        strict = (col > row) if rev else (col < row)
        incl = (col >= row) if rev else (col <= row)
        order = list(range(n_chunks - 1, -1, -1) if rev else range(n_chunks))
        last_row = 0 if rev else CHUNK - 1
        streams = [(c, p) for c in order for p in range(n_pairs)]
        w = {}
        q = {}
        state = {}

        def prep_tile():
            w.update(tok_fn())
            dwda = dwda_ref[...]
            w_logit = decb_refs[dr][...] + _dot_split_rhs(jnp.tanh(dwda), decup_refs[dr][...])
            w["logw"] = (-math.exp(-0.5)) * _sigmoid(w_logit)
            a = _sigmoid(iclb_refs[dr][...] + _dot_split_rhs(dwda, iclup_refs[dr][...]))
            w["kd"] = w["k"] * (1.0 + (a - 1.0) * kaw_ref[...])
            w["bb"] = w["kk"] * a
            l1, l2 = _split2(w["logw"])
            tri = tri_refs[dr][...]
            w["cum"] = _dot(tri, l1) + _dot(tri, l2)
            for p in range(n_pairs):
                state[p] = s_ref[dr, p]

        def prep_chunk(c):
            def run():
                sl = slice(c * CHUNK, (c + 1) * CHUNK)
                cin = w["cum"][sl]
                c_end = cin[last_row:last_row + 1, :]
                g_inv = jnp.exp(-cin)
                g_rat = jnp.exp(c_end - cin)
                rt = w["r"][sl] * jnp.exp(cin)
                kkt = w["kk"][sl] * jnp.exp(cin - w["logw"][sl])
                kt = w["kd"][sl] * g_inv
                bt = w["bb"][sl] * g_inv
                kh = w["kd"][sl] * g_rat
                bh = w["bb"][sl] * g_rat
                vc = w["v"][sl]
                for p in range(n_pairs):
                    ls = slice(p * LANES, (p + 1) * LANES)
                    bk_t = b16(jnp.concatenate([bh[:, ls], kh[:, ls]], axis=0).T)
                    ce_t = jnp.broadcast_to(c_end[:, ls], (LANES, LANES)).T
                    g_col = jnp.exp(jnp.where(lane_lo, ce_t[:HEAD], ce_t[HEAD:]))
                    q[c, p] = dict(sl=sl, ls=ls, g_col=g_col, rt=b16(rt[:, ls]), kkt=kkt[:, ls], kt=kt[:, ls],
                                   bt=bt[:, ls], bk_t=bk_t, vc=vc[:, ls])
            return run

        def per_stream(fn):
            def run():
                for st in streams:
                    fn(q[st])
            return run

        def s_scores(d):
            x = jnp.concatenate([b16(d["kkt"]), d["rt"]], axis=0)
            sbk = _dot(x, jnp.concatenate([pd(d["bt"]), pd(d["kt"])], axis=0), NT)
            sb = sbk[:, :LANES]
            sk = sbk[:, LANES:]
            d["a_b"] = jnp.where(strict, sb[:CHUNK], 0.0)
            d["a_k"] = jnp.where(strict, sk[:CHUNK], 0.0)
            d["m_bk"] = b16(jnp.concatenate([jnp.where(incl, sb[CHUNK:], 0.0),
                                             jnp.where(incl, sk[CHUNK:], 0.0)], axis=1))
            d["n0"] = jnp.where(same16, d["a_b"], 0.0)
            d["vpd"] = pd(d["vc"])

        def s_n2(d):
            d["n2"] = mm(d["n0"], pd(d["n0"]))
            d["akv"] = mm(d["a_k"], d["vpd"])

        def s_n4(d):
            t1 = eye - d["n0"]
            both = _dot(b16(jnp.concatenate([d["n2"], t1], axis=0)), pd(d["n2"]))
            d["n4"] = both[:CHUNK]
            d["t"] = t1 + both[CHUNK:]

        def s_n8(d):
            both = _dot(b16(jnp.concatenate([d["n4"], d["t"]], axis=0)), pd(d["n4"]))
            d["n8"] = both[:CHUNK]
            d["t"] = d["t"] + both[CHUNK:]

        def s_t16(d):
            d["t"] = d["t"] + mm(d["t"], pd(d["n8"]))

        def s_p32(d):
            d["p"] = mm(d["t"], pd(jnp.where(off16, d["a_b"], 0.0)))

        def s_t(d):
            d["t"] = d["t"] - mm(d["p"], pd(d["t"]))

        def s_p64(d):
            d["p"] = mm(d["t"], pd(jnp.where(same32, 0.0, d["a_b"])))

        def s_w(d):
            w12 = _dot(b16(d["t"]), jnp.concatenate([pd(d["kkt"]), pd(d["akv"])], axis=1))
            d["w1r"] = jnp.concatenate([b16(w12[:, :LANES]), d["rt"]], axis=0)
            d["w2"] = w12[:, LANES:]

        def chain(c):
            def run():
                for p in range(n_pairs):
                    d = q[c, p]
                    xs = _dot(d["w1r"], pd(state[p]))
                    d["u"] = -(xs[:CHUNK] + d["w2"])
                    d["ys"] = xs[CHUNK:]
                    full = _dot(d["bk_t"], b16(jnp.concatenate([d["u"], d["vc"]], axis=0)))
                    state[p] = state[p] * d["g_col"] + jnp.where(lane_lo, full[:CHUNK], full[CHUNK:])
            return run

        def s_y(d):
            y = d["ys"] + _dot(d["m_bk"], jnp.concatenate([pd(d["u"]), d["vpd"]], axis=0))
            y_refs[dr][d["sl"], d["ls"]] = y.astype(y_refs[dr].dtype)

        def finish():
            for p in range(n_pairs):
                s_ref[dr, p] = state[p]
            if emit_state:
                @pl.when(t == n_t - 1)
                def _():
                    for p in range(n_pairs):
                        h1, h2, h3 = _split3(state[p])
                        tr = _dot(eye_t, h1, NT) + (_dot(eye_t, h2, NT) + _dot(eye_t, h3, NT))
                        sfin_refs[dr][2 * p] = tr[:HEAD]
                        sfin_refs[dr][2 * p + 1] = tr[HEAD:]

        prep = [prep_tile] + [prep_chunk(c) for c in order]
        work = [per_stream(f) for f in (s_scores, s_n2, s_n4, s_n8, s_t16, s_p32, s_t, s_p64, s_t, s_w)]
        work += [chain(c) for c in order] + [per_stream(s_y), finish]
        return prep, work

    tok_cache = {}

    def tok_f():
        tok_cache["f"] = tokens(in_f, True)
        return tok_cache["f"]

    def tok_r():
        return tok_cache["f"] if shared_tile else tokens(in_r, False)

    prep_f, work_f = direction(0, tok_f, in_f[3])
    prep_r, work_r = direction(1, tok_r, in_r[3])
    for fn in prep_f:
        fn()
    seq_a = work_f
    seq_b = prep_r + work_r
    for i in range(max(len(seq_a), len(seq_b))):
        if i < len(seq_a):
            seq_a[i]()
        if i < len(seq_b):
            seq_b[i]()


def _scan(proj, dwda_blk, conv_w, conv_b, dec_up, dec_base, icl_up, icl_base, k_k, k_a, r_k,
          ones_bd, tri, s0, *, batch, seq, d_a, row_len, emit_state):
    n_t = seq // SCAN_TILE
    n_p = d_a // LANES
    pps = SCAN_PAIRS
    assert n_p % pps == 0
    n_g = n_p // pps
    width = pps * LANES
    tt = SCAN_TILE
    rows = batch * seq

    def tile(rev):
        return (lambda b, t: b * n_t + (n_t - 1 - t)) if rev else (lambda b, t: b * n_t + t)

    def colspec(base, rev):
        ti = tile(rev)
        return pl.BlockSpec((tt, width), lambda b, g, t: (ti(b, t), base + g))

    def tokspecs(rev):
        ti = tile(rev)
        return [colspec(0, rev), colspec(n_g, rev), colspec(2 * n_g, rev),
                pl.BlockSpec((tt, LANES), lambda b, g, t: (ti(b, t), dwda_blk))]

    def wspec(nrow, base=0):
        return pl.BlockSpec((nrow, width), lambda b, g, t, base=base: (0, base + g))

    const2 = lambda shape: pl.BlockSpec(shape, lambda b, g, t: (0, 0))
    in_specs = (tokspecs(False) + tokspecs(True)
                + [wspec(3, 0), wspec(3, n_g), wspec(3, 2 * n_g), wspec(1, 0), wspec(1, n_g), wspec(1, 2 * n_g)]
                + [wspec(LANES)] * 2 + [wspec(1)] * 2 + [wspec(LANES)] * 2 + [wspec(1)] * 2
                + [wspec(1)] * 3 + [const2(ones_bd.shape)] + [const2((tt, tt))] * 2)
    args = ([proj] * 8 + [conv_w] * 3 + [conv_b] * 3
            + [dec_up[0], dec_up[1], dec_base[0], dec_base[1], icl_up[0], icl_up[1], icl_base[0], icl_base[1]]
            + [k_k, k_a, r_k, ones_bd, tri[0], tri[1]])
    has_init = s0 is not None
    if has_init:
        for dr in range(2):
            in_specs.append(pl.BlockSpec((None, None, pps, HEAD, LANES), lambda b, g, t, dr=dr: (b, dr, g, 0, 0)))
            args.append(s0)
    yspec = lambda rev: colspec(0, rev)
    out_specs = [yspec(False), yspec(True), yspec(False)]
    out_shape = [jax.ShapeDtypeStruct((rows, d_a), BF16)] * 2 + [jax.ShapeDtypeStruct((rows, d_a), F32)]
    if emit_state:
        out_specs += [pl.BlockSpec((None, 2 * pps, HEAD, HEAD), lambda b, g, t: (b, g, 0, 0))] * 2
        out_shape += [jax.ShapeDtypeStruct((batch, 2 * n_p, HEAD, HEAD), F32)] * 2
    kern = functools.partial(_scan_kernel, row_len=row_len, has_init=has_init, emit_state=emit_state,
                             n_pairs=pps, shared_tile=(n_t == 1))
    return pl.pallas_call(
        kern,
        grid=(batch, n_g, n_t),
        in_specs=in_specs,
        out_specs=out_specs,
        out_shape=out_shape,
        scratch_shapes=[pltpu.VMEM((2, pps, HEAD, LANES), F32)],
        compiler_params=_cparams(("parallel", "parallel", "arbitrary")),
        name="rwkv7_scan_bidir",
    )(*args)


def _chan_dft_kernel(x_ref, w_ref, o_ref):
    group = w_ref.shape[0]
    w = w_ref[...]
    for g in range(x_ref.shape[1] // group):
        xg = x_ref[:, g * group:(g + 1) * group].astype(BF16)
        o_ref[:, 2 * g * group:2 * (g + 1) * group] = jnp.dot(xg, w, preferred_element_type=F32).astype(BF16)


def _chan_dft(proj, w_cs, xb_blk, d_b, tm):
    r = proj.shape[0]
    group = w_cs.shape[0]
    return pl.pallas_call(
        _chan_dft_kernel,
        grid=(r // tm,),
        in_specs=[pl.BlockSpec((tm, d_b), lambda i: (i, xb_blk)),
                  pl.BlockSpec((group, 2 * group), lambda i: (0, 0))],
        out_specs=pl.BlockSpec((tm, 2 * d_b), lambda i: (i, 0)),
        out_shape=jax.ShapeDtypeStruct((r, 2 * d_b), BF16),
        compiler_params=_cparams(("parallel",)),
        name="fourier_channels",
    )(proj, w_cs)


def _time_dft_kernel(ct_ref, st_ref, z_ref, o_ref):
    ct = ct_ref[...]
    st = st_ref[...]
    group = o_ref.shape[1] // N_GROUPS_B
    for g in range(N_GROUPS_B):
        zc = z_ref[:, 2 * g * group:(2 * g + 1) * group]
        zs = z_ref[:, (2 * g + 1) * group:(2 * g + 2) * group]
        y = jnp.dot(ct, zc, preferred_element_type=F32) - jnp.dot(st, zs, preferred_element_type=F32)
        o_ref[:, g * group:(g + 1) * group] = y.astype(BF16)


def _time_dft(z, ct, st, batch, seq, tm):
    n_m = seq // tm
    d_b = z.shape[1] // 2
    z_mode = pl.Buffered(1) if seq * 2 * d_b * z.dtype.itemsize > RESIDENT_DOUBLE_BUFFER_MAX else None
    return pl.pallas_call(
        _time_dft_kernel,
        grid=(batch, n_m),
        in_specs=[pl.BlockSpec((tm, seq), lambda b, m: (m, 0)),
                  pl.BlockSpec((tm, seq), lambda b, m: (m, 0)),
                  pl.BlockSpec((seq, 2 * d_b), lambda b, m: (b, 0), pipeline_mode=z_mode)],
        out_specs=pl.BlockSpec((tm, d_b), lambda b, m: (b * n_m + m, 0)),
        out_shape=jax.ShapeDtypeStruct((batch * seq, d_b), BF16),
        compiler_params=_cparams(("parallel", "arbitrary")),
        name="fourier_positions",
    )(ct, st, z)


def _dft_mats(n):
    scale = 1.0 / math.sqrt(n)
    m = math.isqrt(n)
    if m * m != n or n <= 1024:
        i = lax.broadcasted_iota(jnp.int32, (n, n), 0)
        j = lax.broadcasted_iota(jnp.int32, (n, n), 1)
        ang = ((i * j) % n).astype(F32) * (2.0 * math.pi / n)
        return jnp.cos(ang) * scale, jnp.sin(ang) * scale
    t = lax.broadcasted_iota(jnp.int32, (m, n), 0)
    f = lax.broadcasted_iota(jnp.int32, (m, n), 1)
    a1 = ((m * t * f) % n).astype(F32) * (2.0 * math.pi / n)
    a2 = ((t * f) % n).astype(F32) * (2.0 * math.pi / n)
    c1, s1 = (jnp.cos(a1) * scale)[:, None, :], (jnp.sin(a1) * scale)[:, None, :]
    c2, s2 = jnp.cos(a2)[None, :, :], jnp.sin(a2)[None, :, :]
    return (c1 * c2 - s1 * s2).reshape(n, n), (s1 * c2 + c1 * s2).reshape(n, n)


def _mix_out_kernel(yf_ref, yr_ref, bonus_ref, dg_ref, gup_ref, lg_ref, lb_ref, avg_ref, yb_ref,
                    ga0_ref, ga1_ref, gb0_ref, gb1_ref, wa_ref, wf_ref, wo_ref, x_ref, mod_ref, o_ref, ya_ref):
    avg = avg_ref[...]
    cw = avg.shape[0]
    sig = _sigmoid(dg_ref[...]).astype(BF16)
    for j in range(ya_ref.shape[1] // cw):
        cols = slice(j * cw, (j + 1) * cw)
        y = yf_ref[:, cols].astype(F32) + yr_ref[:, cols].astype(F32)
        mu = _dot(y.astype(BF16), avg) * (1.0 / HEAD)
        d = y - mu
        var = _dot((d * d).astype(BF16), avg) * (1.0 / HEAD)
        yn = d * lax.rsqrt(var + GN_EPS) * lg_ref[:, cols] + lb_ref[:, cols]
        g = jnp.dot(sig, gup_ref[:, cols], preferred_element_type=F32)
        ya_ref[:, cols] = ((yn + bonus_ref[:, cols]) * g).astype(BF16)

    half = ga0_ref.shape[1]
    ya = ya_ref[...]
    yb = yb_ref[...]
    parts = []
    for q, (ga_ref, gb_ref) in enumerate(((ga0_ref, gb0_ref), (ga1_ref, gb1_ref))):
        cols = slice(q * half, (q + 1) * half)
        pa = jnp.dot(ya, wa_ref[:, cols], preferred_element_type=F32)
        pb = jnp.dot(yb, wf_ref[:, cols], preferred_element_type=F32)
        parts.append((_sigmoid(ga_ref[...]) * pa + _sigmoid(gb_ref[...]) * pb).astype(BF16))
    mixed = jnp.concatenate(parts, axis=1)
    p = jnp.dot(mixed, wo_ref[...], preferred_element_type=F32)
    o_ref[...] = x_ref[...] + mod_ref[2:3, :] * p


def _mix_out(yf, yr, bonus, dg_blk, gate_up, lnx_g, lnx_b, avg, yb, proj, c_gates, w_a, w_f, w_o, x, mod,
             rows_per_seg, tm):
    r, d = x.shape
    d_a = yf.shape[1]
    d_b = yb.shape[1]
    lg = gate_up.shape[0]
    cw = avg.shape[0]
    half = d // 2
    assert c_gates % half == 0
    gblk = c_gates // half
    tps = rows_per_seg // tm
    resident = lambda shape: pl.BlockSpec(shape, lambda i: (0, 0), pipeline_mode=pl.Buffered(1))
    gate = lambda k: pl.BlockSpec((tm, half), lambda i, k=k: (i, gblk + k))
    tile_a = pl.BlockSpec((tm, d_a), lambda i: (i, 0))
    vec_a = pl.BlockSpec((1, d_a), lambda i: (0, 0))
    return pl.pallas_call(
        _mix_out_kernel,
        grid=(r // tm,),
        in_specs=[tile_a, tile_a, tile_a,
                  pl.BlockSpec((tm, lg), lambda i: (i, dg_blk)),
                  resident((lg, d_a)), vec_a, vec_a, resident((cw, cw)),
                  pl.BlockSpec((tm, d_b), lambda i: (i, 0)),
                  gate(0), gate(1), gate(2), gate(3),
                  resident((d_a, d)), resident((d_b, d)), resident((d, d)),
                  pl.BlockSpec((tm, d), lambda i: (i, 0)),
                  pl.BlockSpec((None, 6, d), lambda i: (i // tps, 0, 0))],
        out_specs=pl.BlockSpec((tm, d), lambda i: (i, 0)),
        out_shape=jax.ShapeDtypeStruct((r, d), F32),
        scratch_shapes=[pltpu.VMEM((tm, d_a), BF16)],
        compiler_params=_cparams(("parallel",)),
        name="headnorm_merge_outproj",
    )(yf, yr, bonus, proj, gate_up, lnx_g.reshape(1, d_a), lnx_b.reshape(1, d_a), avg, yb,
      proj, proj, proj, proj, w_a, w_f, w_o, x, mod)


def _ffn_in_kernel(x_ref, mod_ref, g_ref, wg_ref, wv_ref, cwg_ref, cwv_ref, cbg_ref, cbv_ref, o_ref, h_ref,
                   *, row_len):
    @pl.when(pl.program_id(1) == 0)
    def _():
        h = _rms_mod(x_ref[...], g_ref[...], mod_ref[3:4, :], mod_ref[4:5, :])
        h_ref[...] = h.astype(BF16)

    tm, tn = o_ref.shape
    sub_n = MXU_COLS
    sub_m = min(tm, FFN_SUB_ROWS)
    assert sub_m % row_len == 0
    pos = lax.broadcasted_iota(jnp.int32, (sub_m, sub_n), 0) & (row_len - 1)
    blocks = [(slice(m * sub_m, (m + 1) * sub_m), slice(q * sub_n, (q + 1) * sub_n))
              for m in range(tm // sub_m) for q in range(tn // sub_n)]

    def matmuls(blk):
        rows, cols = blk
        h = h_ref[rows, :]
        return (jnp.dot(h, wg_ref[:, cols], preferred_element_type=F32),
                jnp.dot(h, wv_ref[:, cols], preferred_element_type=F32))

    def epilogue(blk, ug, uv):
        rows, cols = blk
        ug = _conv3(ug, cwg_ref.at[:, cols], cbg_ref.at[:, cols], pos, row_len)
        uv = _conv3(uv, cwv_ref.at[:, cols], cbv_ref.at[:, cols], pos, row_len)
        o_ref[rows, cols] = (ug * _sigmoid(ug) * uv).astype(BF16)

    cur = matmuls(blocks[0])
    for i, blk in enumerate(blocks):
        nxt = matmuls(blocks[i + 1]) if i + 1 < len(blocks) else None
        epilogue(blk, *cur)
        cur = nxt


def _ffn_in(x, mod, g, w_gv, cw_gv, cb_gv, rows_per_seg, row_len, tm, tn):
    r, d = x.shape
    ffp = w_gv[0].shape[1]
    tps = rows_per_seg // tm
    wspec = pl.BlockSpec((d, tn), lambda i, j: (0, j))
    cwspec = pl.BlockSpec((3, tn), lambda i, j: (0, j))
    cbspec = pl.BlockSpec((1, tn), lambda i, j: (0, j))
    return pl.pallas_call(
        functools.partial(_ffn_in_kernel, row_len=row_len),
        grid=(r // tm, ffp // tn),
        in_specs=[pl.BlockSpec((tm, d), lambda i, j: (i, 0)),
                  pl.BlockSpec((None, 6, d), lambda i, j: (i // tps, 0, 0)),
                  pl.BlockSpec((1, d), lambda i, j: (0, 0)),
                  wspec, wspec, cwspec, cwspec, cbspec, cbspec],
        out_specs=pl.BlockSpec((tm, tn), lambda i, j: (i, j)),
        out_shape=jax.ShapeDtypeStruct((r, ffp), BF16),
        scratch_shapes=[pltpu.VMEM((tm, d), BF16)],
        compiler_params=_cparams(("parallel", "arbitrary")),
        name="ffn_in_conv_gate",
    )(x, mod, g.reshape(1, d), *w_gv, *cw_gv, *cb_gv)


def _ffn_down_kernel(a_ref, w_ref, x_ref, mod_ref, g_ref, o_ref):
    x2 = x_ref[...] + mod_ref[5:6, :] * jnp.dot(a_ref[...], w_ref[...], preferred_element_type=F32)
    ms = jnp.mean(x2 * x2, axis=-1, keepdims=True)
    o_ref[...] = x2 * lax.rsqrt(ms + RMS_EPS) * g_ref[...]


def _ffn_down(act, w, x, mod, g, rows_per_seg, tm):
    r, d = x.shape
    ffp = act.shape[1]
    tps = rows_per_seg // tm
    return pl.pallas_call(
        _ffn_down_kernel,
        grid=(r // tm,),
        in_specs=[pl.BlockSpec((tm, ffp), lambda i: (i, 0)),
                  pl.BlockSpec((ffp, d), lambda i: (0, 0), pipeline_mode=pl.Buffered(1)),
                  pl.BlockSpec((tm, d), lambda i: (i, 0)),
                  pl.BlockSpec((None, 6, d), lambda i: (i // tps, 0, 0)),
                  pl.BlockSpec((1, d), lambda i: (0, 0))],
        out_specs=pl.BlockSpec((tm, d), lambda i: (i, 0)),
        out_shape=jax.ShapeDtypeStruct((r, d), F32),
        compiler_params=_cparams(("parallel",)),
        name="ffn_down_final_norm",
    )(act, w, x, mod, g.reshape(1, d))


def _pair_layout(s):
    lead = s.shape[:-3]
    h = s.shape[-3]
    s = s.reshape(lead + (h // 2, 2, HEAD, HEAD))
    s = jnp.swapaxes(s, -3, -2)
    return s.reshape(lead + (h // 2, HEAD, 2 * HEAD))


def _layer(x, mod, s0, batch, seq, row_len, wts, emit_state):
    d = x.shape[1]
    rows = batch * seq
    rows_per_seg = rows // mod.shape[0]
    d_a = wts["d_a"]
    d_b = wts["d_b"]
    lora_w = wts["lora_w"]
    lora_g = wts["lora_g"]
    c_xb = 3 * d_a
    c_gates = c_xb + d_b
    c_dw = c_gates + 2 * d
    c_dg = c_dw + LANES
    assert lora_w == HEAD and wts["lora_a"] == HEAD and lora_g == LANES
    assert seq % SCAN_TILE == 0 and SCAN_TILE % row_len == 0 and row_len & (row_len - 1) == 0

    tm = _pick(rows_per_seg, (1024, 512, 256))
    n_in = wts["w_in"].shape[1]
    proj = _inproj(x, mod, wts["norm_mix_g"], wts["w_in"], rows_per_seg, tm, _pick(n_in, (1280, 768, 512, 256)))

    outs = _scan(proj, c_dw // LANES, wts["rkv_conv_w"], wts["rkv_conv_b"], wts["decay_up"], wts["decay_base"],
                 wts["iclr_up"], wts["iclr_base"], wts["k_k"], wts["k_a"], wts["r_k"], wts["avg_bd"], wts["tri"],
                 s0, batch=batch, seq=seq, d_a=d_a, row_len=row_len, emit_state=emit_state)
    ys, bonus, states = outs[:2], outs[2], list(outs[3:])

    assert c_xb % d_b == 0
    z = _chan_dft(proj, wts["chan_cs"], c_xb // d_b, d_b, _pick(rows, (512, 256)))
    ct, st = wts["time_cs"][seq]
    yb = _time_dft(z, ct, st, batch, seq, _pick(seq, (512, 256)))

    x1 = _mix_out(ys[0], ys[1], bonus, c_dg // lora_g, wts["gate_up"], wts["lnx_g"], wts["lnx_b"], wts["avg_bd"],
                  yb, proj, c_gates, wts["w_out_a"], wts["w_fourier"], wts["w_out"], x, mod, rows_per_seg, 256)

    act = _ffn_in(x1, mod, wts["norm_ffn_g"], wts["ffn_w_in"], wts["ffn_conv_w"], wts["ffn_conv_b"],
                  rows_per_seg, row_len, tm, 512)
    y = _ffn_down(act, wts["ffn_w_down"], x1, mod, wts["final_norm_g"], rows_per_seg, 256)
    return y, states


def _tri_blockdiag(n, rev):
    i = np.arange(n)[:, None]
    j = np.arange(n)[None, :]
    same = (i // CHUNK) == (j // CHUNK)
    m = same & ((j >= i) if rev else (j <= i))
    return jnp.asarray(m.astype(np.float32), dtype=BF16)


def _blockdiag_ones(n):
    i = np.arange(n)[:, None] // HEAD
    j = np.arange(n)[None, :] // HEAD
    return jnp.asarray((i == j).astype(np.float32), dtype=BF16)


def kernel(x_prompt, x_sample, state_rwkv, c, c_ctx, ada_w, ada_b, norm_mix_g, w_in, rkv_conv_w, rkv_conv_b,
           decay_up, decay_base, iclr_up, iclr_base, gate_up, k_k, k_a, r_k, lnx_g, lnx_b, w_out_a, w_fourier,
           w_out, norm_ffn_g, ffn_w_in, ffn_conv_w, ffn_conv_b, ffn_w_down, final_norm_g):
    batch, ctx_len, d = x_prompt.shape
    dec_batch, dec_seq, _ = x_sample.shape
    depth = ada_w.shape[0]
    assert depth == 1
    d_a = w_out_a.shape[1]
    d_b = w_fourier.shape[1]
    d_ff = ffn_w_down.shape[1]
    ffp = -(-d_ff // 512) * 512
    lora_w = decay_up.shape[2]
    lora_a = iclr_up.shape[2]
    lora_g = gate_up.shape[1]
    group = d_b // N_GROUPS_B

    cvec = jnp.concatenate([c_ctx[None, :], c], axis=0).astype(F32)
    n_vec = cvec.shape[0]
    cvec = jnp.pad(cvec, ((0, 8 - n_vec), (0, 0)))
    mod = _modulation(cvec, ada_w[0].astype(F32), ada_b[0].astype(F32))[:n_vec].reshape(n_vec, 6, d)

    l = 0
    zpad_w = jnp.zeros((2, LANES - lora_w, d_a), F32)
    zpad_a = jnp.zeros((2, LANES - lora_a, d_a), F32)
    ffn_in = ffn_w_in[l]
    pad_ff = ((0, 0), (0, ffp - d_ff))
    cc, sc = _dft_mats(group)
    wts = {
        "d_a": d_a, "d_b": d_b, "lora_w": lora_w, "lora_a": lora_a, "lora_g": lora_g,
        "norm_mix_g": norm_mix_g[l], "w_in": w_in[l].astype(BF16),
        "rkv_conv_w": rkv_conv_w[l], "rkv_conv_b": rkv_conv_b[l].reshape(1, -1),
        "decay_up": jnp.concatenate([decay_up[l], zpad_w], axis=1),
        "iclr_up": jnp.concatenate([zpad_a, iclr_up[l]], axis=1),
        "decay_base": decay_base[l].reshape(2, 1, d_a), "iclr_base": iclr_base[l].reshape(2, 1, d_a),
        "k_k": k_k[l].reshape(1, d_a), "k_a": k_a[l].reshape(1, d_a), "r_k": r_k[l].reshape(1, d_a),
        "gate_up": gate_up[l].astype(BF16), "lnx_g": lnx_g[l], "lnx_b": lnx_b[l],
        "w_out_a": w_out_a[l].astype(BF16), "w_fourier": w_fourier[l].astype(BF16), "w_out": w_out[l].astype(BF16),
        "norm_ffn_g": norm_ffn_g[l],
        "ffn_w_in": tuple(jnp.pad(h.astype(BF16), pad_ff) for h in (ffn_in[:, :d_ff], ffn_in[:, d_ff:])),
        "ffn_conv_w": tuple(jnp.pad(h, pad_ff) for h in (ffn_conv_w[l][:, :d_ff], ffn_conv_w[l][:, d_ff:])),
        "ffn_conv_b": tuple(jnp.pad(h, pad_ff) for h in (ffn_conv_b[l][None, :d_ff], ffn_conv_b[l][None, d_ff:])),
        "ffn_w_down": jnp.pad(ffn_w_down[l].astype(BF16), ((0, ffp - d_ff), (0, 0))),
        "final_norm_g": final_norm_g,
        "avg_bd": _blockdiag_ones(MXU_COLS),
        "tri": (_tri_blockdiag(SCAN_TILE, False), _tri_blockdiag(SCAN_TILE, True)),
        "chan_cs": jnp.concatenate([cc, sc], axis=1).astype(BF16),
        "time_cs": {n: tuple(m.astype(BF16) for m in _dft_mats(n)) for n in {ctx_len, dec_seq}},
    }

    y_ctx, st_ctx = _layer(x_prompt.reshape(batch * ctx_len, d), mod[:1], None, batch, ctx_len, ctx_len,
                           wts, emit_state=True)
    s0 = _pair_layout(jnp.swapaxes(state_rwkv[:, l].astype(F32), -1, -2))
    y_lat, _ = _layer(x_sample.reshape(dec_batch * dec_seq, d), mod[1:], s0, dec_batch, dec_seq, GRID_W,
                      wts, emit_state=False)

    new_state = jnp.stack(st_ctx, axis=1)[:, None].astype(x_prompt.dtype)
    return (y_ctx.reshape(batch, ctx_len, d), y_lat.reshape(dec_batch, dec_seq, d), new_state)
```
